```python
import math
import jax, jax.numpy as jnp
from jax import lax
import numpy as np

D_MODEL = 1024
BATCH = 4
SEQ = 8192
DEPTH = 1

RET_WIDTH = D_MODEL // 2
RET_HEAD_DIM = 128
RET_HEADS = RET_WIDTH // RET_HEAD_DIM
RET_CHUNK = 128
ATTN_WIDTH = D_MODEL - RET_WIDTH
ATTN_HEAD_DIM = 64
ATTN_HEADS = ATTN_WIDTH // ATTN_HEAD_DIM
DILATED_PATTERNS = ((128, 1), (512, 4), (2048, 16))
ATTN_BLOCK = 128
ROPE_THETA = 10000.0
D_FF = 2816
N_MOD = 9
IN_COLS = 4 * RET_WIDTH + 3 * ATTN_WIDTH
EPS = 1e-6

kernel_name = "hybrid_retention_dilated_macaron_adaln"


def rms_norm(x, g):
    xf = x.astype(jnp.float32)
    y = xf * lax.rsqrt(jnp.mean(xf * xf, axis=-1, keepdims=True) + EPS)
    return (y * g.astype(jnp.float32)).astype(x.dtype)


def modulate(h, shift, scale):
    return h * (1.0 + scale[:, None, :]) + shift[:, None, :]


def swiglu(h, w_in, w_out):
    a, b = jnp.split(h @ w_in, 2, axis=-1)
    return (jax.nn.silu(a) * b) @ w_out


def rope(x, pos):
    d = x.shape[-1]
    inv = ROPE_THETA ** (-jnp.arange(0, d, 2, dtype=jnp.float32) / d)
    ang = pos.astype(jnp.float32)[:, None] * inv[None, :]
    cos = jnp.cos(ang)[None, :, None, :]
    sin = jnp.sin(ang)[None, :, None, :]
    xf = x.astype(jnp.float32)
    x1, x2 = jnp.split(xf, 2, axis=-1)
    return jnp.concatenate([x1 * cos - x2 * sin, x2 * cos + x1 * sin], axis=-1).astype(x.dtype)


def retention(q, k, v):
    B, T, H, d = q.shape
    C = RET_CHUNK
    N = T // C
    log_g = jnp.log1p(-jnp.exp2(-5.0 - jnp.arange(H, dtype=jnp.float32)))

    def chunks(a):
        return a.astype(jnp.float32).transpose(0, 2, 1, 3).reshape(B, H, N, C, d)

    qc = chunks(q)
    kc = chunks(k) * (d ** -0.5)
    vc = chunks(v)
    i = jnp.arange(C, dtype=jnp.float32)
    diff = i[:, None] - i[None, :]
    decay = jnp.where(diff >= 0, jnp.exp(jnp.maximum(diff, 0.0)[None] * log_g[:, None, None]), 0.0)
    inner = jnp.einsum('bhnid,bhnjd->bhnij', qc, kc) * decay[None, :, None]
    inner = jnp.einsum('bhnij,bhnje->bhnie', inner, vc)

    k_decay = jnp.exp((C - 1.0 - i)[None, :] * log_g[:, None])
    kv = jnp.einsum('bhnjd,bhnje->nbhde', kc * k_decay[None, :, None, :, None], vc)
    chunk_decay = jnp.exp(C * log_g)[None, :, None, None]

    def step(state, kv_n):
        return chunk_decay * state + kv_n, state

    _, prev = lax.scan(step, jnp.zeros((B, H, d, d), jnp.float32), kv)
    q_decay = jnp.exp((i + 1.0)[None, :] * log_g[:, None])
    cross = jnp.einsum('bhnid,nbhde->bhnie', qc * q_decay[None, :, None, :, None], prev)
    o = (inner + cross).reshape(B, H, T, d)
    mu = jnp.mean(o, axis=-1, keepdims=True)
    var = jnp.mean(jnp.square(o - mu), axis=-1, keepdims=True)
    o = (o - mu) * lax.rsqrt(var + EPS)
    return o.transpose(0, 2, 1, 3)


def dilated_branch(q, k, v, window, dilation):
    B, T, H, dh = q.shape
    r = dilation
    steps = window // r
    L = T // r
    nb = -(-L // ATTN_BLOCK)
    Lp = nb * ATTN_BLOCK

    def split(a):
        a = a.reshape(B, L, r, H, dh).transpose(0, 2, 3, 1, 4)
        a = jnp.pad(a, ((0, 0), (0, 0), (0, 0), (0, Lp - L), (0, 0)))
        return a.reshape(B, r, H, nb, ATTN_BLOCK, dh)

    def with_prev(a):
        prev = jnp.pad(a[:, :, :, :-1], ((0, 0), (0, 0), (0, 0), (1, 0), (0, 0), (0, 0)))
        return jnp.concatenate([prev, a], axis=4)

    qb = split(q).astype(jnp.float32)
    kb = with_prev(split(k)).astype(jnp.float32)
    vb = with_prev(split(v)).astype(jnp.float32)
    s = jnp.einsum('bshnqd,bshnkd->bshnqk', qb, kb) * (dh ** -0.5)
    qi = jnp.arange(ATTN_BLOCK)[:, None]
    kj = jnp.arange(2 * ATTN_BLOCK)[None, :]
    dist = ATTN_BLOCK + qi - kj
    key_pos = (jnp.arange(nb)[:, None, None] - 1) * ATTN_BLOCK + kj[None]
    mask = (dist >= 0)[None] & (dist <= steps)[None] & (key_pos >= 0)
    s = jnp.where(mask, s, -jnp.inf)
    lse = jax.nn.logsumexp(s, axis=-1)
    p = jnp.exp(s - lse[..., None])
    o = jnp.einsum('bshnqk,bshnkd->bshnqd', p, vb)
    o = o.reshape(B, r, H, Lp, dh)[:, :, :, :L].transpose(0, 3, 1, 2, 4).reshape(B, T, H, dh)
    lse = lse.reshape(B, r, H, Lp)[:, :, :, :L].transpose(0, 3, 1, 2).reshape(B, T, H)
    return o, lse


def dilated_mixture(q, k, v):
    outs, lses = [], []
    for window, dilation in DILATED_PATTERNS:
        o, l = dilated_branch(q, k, v, window, dilation)
        outs.append(o)
        lses.append(l)
    w = jax.nn.softmax(jnp.stack(lses, axis=0), axis=0)
    return jnp.sum(w[..., None] * jnp.stack(outs, axis=0), axis=0)


def token_mixer(h, w_in, ret_gn_g, attn_norm_g, w_out):
    B, T, _ = h.shape
    proj = h @ w_in
    RW, AW = RET_WIDTH, ATTN_WIDTH
    rq, rk, rv, rg, aq, ak, av = jnp.split(
        proj, [RW, 2 * RW, 3 * RW, 4 * RW, 4 * RW + AW, 4 * RW + 2 * AW], axis=-1)
    pos = jnp.arange(T)

    def rh(a):
        return a.reshape(B, T, RET_HEADS, RET_HEAD_DIM)

    def ah(a):
        return a.reshape(B, T, ATTN_HEADS, ATTN_HEAD_DIM)

    ret = retention(rope(rh(rq), pos), rope(rh(rk), pos), rh(rv)).reshape(B, T, RW)
    ret = ret * ret_gn_g.astype(jnp.float32) * jax.nn.silu(rg.astype(jnp.float32))
    att = dilated_mixture(rope(ah(aq), pos), rope(ah(ak), pos), ah(av))
    att = att * lax.rsqrt(jnp.mean(att * att, axis=-1, keepdims=True) + EPS)
    att = att.reshape(B, T, AW) * attn_norm_g.astype(jnp.float32)
    y = jnp.concatenate([ret, att], axis=-1).astype(h.dtype)
    return y @ w_out


def setup_inputs(seed: int = 0) -> dict:
    key = jax.random.key(seed)
    ks = jax.random.split(key, 20)
    f32 = jnp.float32
    D, F, L = D_MODEL, D_FF, DEPTH

    def nrm(k, shape, scale):
        return jax.random.normal(k, shape, f32) * scale

    return {
        "x": nrm(ks[0], (BATCH, SEQ, D), 1.0),
        "c": nrm(ks[1], (BATCH, D), 1.0),
        "w_ada": nrm(ks[2], (L, D, N_MOD * D), 0.5 * D ** -0.5),
        "b_ada": nrm(ks[3], (L, N_MOD * D), 0.01),
        "norm1_g": 1.0 + nrm(ks[4], (L, D), 0.02),
        "ffn1_w_in": nrm(ks[5], (L, D, 2 * F), D ** -0.5),
        "ffn1_w_out": nrm(ks[6], (L, F, D), F ** -0.5),
        "norm_mix_g": 1.0 + nrm(ks[7], (L, D), 0.02),
        "w_in_mix": nrm(ks[8], (L, D, IN_COLS), D ** -0.5),
        "ret_gn_g": 1.0 + nrm(ks[9], (L, RET_WIDTH), 0.02),
        "attn_norm_g": 1.0 + nrm(ks[10], (L, ATTN_WIDTH), 0.02),
        "w_out_mix": nrm(ks[11], (L, D, D), D ** -0.5),
        "norm2_g": 1.0 + nrm(ks[12], (L, D), 0.02),
        "ffn2_w_in": nrm(ks[13], (L, D, 2 * F), D ** -0.5),
        "ffn2_w_out": nrm(ks[14], (L, F, D), F ** -0.5),
        "norm_f_g": 1.0 + nrm(ks[15], (D,), 0.02),
    }


def reference(x, c, w_ada, b_ada, norm1_g, ffn1_w_in, ffn1_w_out, norm_mix_g, w_in_mix,
              ret_gn_g, attn_norm_g, w_out_mix, norm2_g, ffn2_w_in, ffn2_w_out, norm_f_g):
    c_act = jax.nn.silu(c)
    for l in range(DEPTH):
        mod = c_act @ w_ada[l] + b_ada[l]
        sh1, sc1, g1, sh2, sc2, g2, sh3, sc3, g3 = jnp.split(mod, N_MOD, axis=-1)
        h = modulate(rms_norm(x, norm1_g[l]), sh1, sc1)
        x = x + 0.5 * g1[:, None, :] * swiglu(h, ffn1_w_in[l], ffn1_w_out[l])
        h = modulate(rms_norm(x, norm_mix_g[l]), sh2, sc2)
        x = x + g2[:, None, :] * token_mixer(h, w_in_mix[l], ret_gn_g[l], attn_norm_g[l], w_out_mix[l])
        h = modulate(rms_norm(x, norm2_g[l]), sh3, sc3)
        x = x + 0.5 * g3[:, None, :] * swiglu(h, ffn2_w_in[l], ffn2_w_out[l])
    return rms_norm(x, norm_f_g)
```

```python
import functools
import math

import jax
import jax.numpy as jnp
from jax import lax
from jax.experimental import pallas as pl
from jax.experimental.pallas import tpu as pltpu

D_MODEL = 1024
RET_WIDTH = 512
RET_HEAD_DIM = 128
RET_HEADS = 4
RET_CHUNK = 128
ATTN_WIDTH = 512
ATTN_HEAD_DIM = 64
DILATED_PATTERNS = ((128, 1), (512, 4), (2048, 16))
ATTN_BLOCK = 128
ROPE_THETA = 10000.0
D_FF = 2816
N_MOD = 9
IN_COLS = 4 * RET_WIDTH + 3 * ATTN_WIDTH
EPS = 1e-6

LANES = 128
VMEM_LIMIT = 56 * 1024 * 1024

FFN_TM = 512
FFN_FC = 256
PROJ_TM = 512
RET_TC = 512
ATTN_TT = ATTN_BLOCK * DILATED_PATTERNS[-1][1]

BF16 = jnp.bfloat16
F32 = jnp.float32


def _sigmoid(x):
    return 1.0 / (1.0 + jnp.exp(-x))


def _norm_mod(x, g, shift, scale):
    ms = jnp.mean(x * x, axis=-1, keepdims=True)
    return (x * lax.rsqrt(ms + EPS) * g) * (1.0 + scale) + shift


def _mod_kernel(c_ref, w_ref, b_ref, o_ref):
    c = c_ref[...]
    ca = c * _sigmoid(c)
    o_ref[...] = jnp.dot(ca, w_ref[...], preferred_element_type=F32) + b_ref[...]


def _adaln_mod(c, w, b):
    bsz, d = c.shape
    n = w.shape[1]
    tn = 1024
    return pl.pallas_call(
        _mod_kernel,
        out_shape=jax.ShapeDtypeStruct((bsz, n), F32),
        grid=(n // tn,),
        in_specs=[
            pl.BlockSpec((bsz, d), lambda j: (0, 0)),
            pl.BlockSpec((d, tn), lambda j: (0, j)),
            pl.BlockSpec((1, tn), lambda j: (0, j)),
        ],
        out_specs=pl.BlockSpec((bsz, tn), lambda j: (0, j)),
        compiler_params=pltpu.CompilerParams(dimension_semantics=("arbitrary",)),
        name="adaln_mod",
    )(c, w, b.reshape(1, n))


def _ffn_kernel(x_ref, mod_ref, g_ref, win_ref, wout_ref, gf_ref, o_ref, h_ref, act_ref, *, sub, final_norm):
    x = x_ref[...]
    shift = mod_ref[3 * sub:3 * sub + 1, :]
    scale = mod_ref[3 * sub + 1:3 * sub + 2, :]
    gate = mod_ref[3 * sub + 2:3 * sub + 3, :]
    h_ref[...] = _norm_mod(x, g_ref[...], shift, scale).astype(BF16)
    for j in range(D_FF // FFN_FC):
        h = h_ref[...]
        a = jnp.dot(h, win_ref[:, j * FFN_FC:(j + 1) * FFN_FC], preferred_element_type=F32)
        b = jnp.dot(h, win_ref[:, D_FF + j * FFN_FC:D_FF + (j + 1) * FFN_FC], preferred_element_type=F32)
        act_ref[:, j * FFN_FC:(j + 1) * FFN_FC] = (a * _sigmoid(a) * b).astype(BF16)
    y = jnp.dot(act_ref[...], wout_ref[...], preferred_element_type=F32)
    xn = x + 0.5 * gate * y
    if final_norm:
        ms = jnp.mean(xn * xn, axis=-1, keepdims=True)
        xn = xn * lax.rsqrt(ms + EPS) * gf_ref[...]
    o_ref[...] = xn


def _ffn(x, mod, g, w_in, w_out, g_final, *, sub, final_norm):
    bsz, t, d = x.shape
    tm = FFN_TM
    kern = functools.partial(_ffn_kernel, sub=sub, final_norm=final_norm)
    return pl.pallas_call(
        kern,
        out_shape=jax.ShapeDtypeStruct((bsz, t, d), F32),
        grid=(bsz, t // tm),
        in_specs=[
            pl.BlockSpec((None, tm, d), lambda b, i: (b, i, 0)),
            pl.BlockSpec((None, N_MOD, d), lambda b, i: (b, 0, 0)),
            pl.BlockSpec((1, d), lambda b, i: (0, 0)),
            pl.BlockSpec((d, 2 * D_FF), lambda b, i: (0, 0), pipeline_mode=pl.Buffered(1)),
            pl.BlockSpec((D_FF, d), lambda b, i: (0, 0), pipeline_mode=pl.Buffered(1)),
            pl.BlockSpec((1, d), lambda b, i: (0, 0)),
        ],
        out_specs=pl.BlockSpec((None, tm, d), lambda b, i: (b, i, 0)),
        scratch_shapes=[pltpu.VMEM((tm, d), BF16), pltpu.VMEM((tm, D_FF), BF16)],
        compiler_params=pltpu.CompilerParams(
            dimension_semantics=("arbitrary", "arbitrary"), vmem_limit_bytes=VMEM_LIMIT),
        name=f"ffn{sub // 2 + 1}",
    )(x, mod, g, w_in, w_out, g_final)


def _proj_kernel(x_ref, mod_ref, g_ref, w_ref, cr_ref, sr_ref, ca_ref, sa_ref, sb_ref,
                 rq_ref, rk_ref, rv_ref, rg_ref, aq_ref, ak_ref, av_ref, h_ref):
    x = x_ref[...]
    h_ref[...] = _norm_mod(x, g_ref[...], mod_ref[3:4, :], mod_ref[4:5, :]).astype(BF16)

    def proj(c):
        return jnp.dot(h_ref[...], w_ref[:, c * 512:(c + 1) * 512], preferred_element_type=F32)

    def rope_ret(p):
        cos, sin = cr_ref[...], sr_ref[...]
        outs = []
        for hh in range(RET_HEADS):
            xh = p[:, hh * LANES:(hh + 1) * LANES]
            outs.append(xh * cos + pltpu.roll(xh, 64, 1) * sin)
        return jnp.concatenate(outs, axis=1)

    def rope_attn(p):
        cos, sin_lo, sin_hi = ca_ref[...], sa_ref[...], sb_ref[...]
        outs = []
        for hh in range(ATTN_WIDTH // LANES):
            xh = p[:, hh * LANES:(hh + 1) * LANES]
            outs.append(xh * cos + pltpu.roll(xh, 96, 1) * sin_lo + pltpu.roll(xh, 32, 1) * sin_hi)
        return jnp.concatenate(outs, axis=1)

    rq_ref[...] = rope_ret(proj(0)).astype(BF16)
    rk_ref[...] = (rope_ret(proj(1)) * (RET_HEAD_DIM ** -0.5)).astype(BF16)
    rv_ref[...] = proj(2).astype(BF16)
    rg_ref[...] = proj(3).astype(BF16)
    aq_ref[...] = rope_attn(proj(4)) * (ATTN_HEAD_DIM ** -0.5)
    ak_ref[...] = rope_attn(proj(5))
    av_ref[...] = proj(6)


def _rope_tables(t):
    pos = jnp.arange(t, dtype=F32)[:, None]
    inv_r = ROPE_THETA ** (-jnp.arange(0, RET_HEAD_DIM, 2, dtype=F32) / RET_HEAD_DIM)
    ang_r = pos * inv_r[None, :]
    cos_r = jnp.concatenate([jnp.cos(ang_r)] * 2, axis=1)
    sin_r = jnp.concatenate([-jnp.sin(ang_r), jnp.sin(ang_r)], axis=1)
    inv_a = ROPE_THETA ** (-jnp.arange(0, ATTN_HEAD_DIM, 2, dtype=F32) / ATTN_HEAD_DIM)
    ang_a = pos * inv_a[None, :]
    ca, sa = jnp.cos(ang_a), jnp.sin(ang_a)
    z = jnp.zeros_like(sa)
    cos_a = jnp.concatenate([ca] * 4, axis=1)
    sin_lo = jnp.concatenate([-sa, z, -sa, z], axis=1)
    sin_hi = jnp.concatenate([z, sa, z, sa], axis=1)
    return cos_r, sin_r, cos_a, sin_lo, sin_hi


def _mixer_in_proj(x, mod, g, w):
    bsz, t, d = x.shape
    tm = PROJ_TM
    tables = _rope_tables(t)
    tok = lambda b, i: (b, i, 0)
    tab = pl.BlockSpec((tm, LANES), lambda b, i: (i, 0))
    out_b = jax.ShapeDtypeStruct((bsz, t, 512), BF16)
    out_f = jax.ShapeDtypeStruct((bsz, t, 512), F32)
    return pl.pallas_call(
        _proj_kernel,
        out_shape=[out_b, out_b, out_b, out_b, out_f, out_f, out_f],
        grid=(bsz, t // tm),
        in_specs=[
            pl.BlockSpec((None, tm, d), tok),
            pl.BlockSpec((None, N_MOD, d), lambda b, i: (b, 0, 0)),
            pl.BlockSpec((1, d), lambda b, i: (0, 0)),
            pl.BlockSpec((d, IN_COLS), lambda b, i: (0, 0), pipeline_mode=pl.Buffered(1)),
            tab, tab, tab, tab, tab,
        ],
        out_specs=[pl.BlockSpec((None, tm, 512), tok)] * 7,
        scratch_shapes=[pltpu.VMEM((tm, d), BF16)],
        compiler_params=pltpu.CompilerParams(
            dimension_semantics=("arbitrary", "arbitrary"), vmem_limit_bytes=VMEM_LIMIT),
        name="mixer_in_proj",
    )(x, mod, g, w, *tables)


def _ret_kernel(q_ref, k_ref, v_ref, g_ref, gn_ref, o_ref, state_ref):
    c = RET_CHUNK

    @pl.when(pl.program_id(1) == 0)
    def _():
        state_ref[...] = jnp.zeros_like(state_ref)

    ii = lax.broadcasted_iota(jnp.int32, (c, c), 0)
    jj = lax.broadcasted_iota(jnp.int32, (c, c), 1)
    diff = (ii - jj).astype(F32)
    ivec = lax.broadcasted_iota(jnp.int32, (c, 1), 0).astype(F32)
    for hh in range(RET_HEADS):
        log_g = math.log1p(-(2.0 ** (-5.0 - hh)))
        decay = jnp.where(diff >= 0, jnp.exp(jnp.maximum(diff, 0.0) * log_g), 0.0)
        k_decay = jnp.exp((c - 1.0 - ivec) * log_g)
        q_decay = jnp.exp((ivec + 1.0) * log_g)
        chunk_decay = math.exp(c * log_g)
        cols = slice(hh * LANES, (hh + 1) * LANES)
        gn = gn_ref[:, cols]
        for n in range(RET_TC // c):
            rows = slice(n * c, (n + 1) * c)
            q = q_ref[rows, cols]
            k = k_ref[rows, cols]
            v = v_ref[rows, cols]
            s = lax.dot_general(q, k, (((1,), (1,)), ((), ())), preferred_element_type=F32) * decay
            inner = jnp.dot(s.astype(BF16), v, preferred_element_type=F32)
            st = state_ref[hh]
            qd = (q.astype(F32) * q_decay).astype(BF16)
            cross = jnp.dot(qd, st.astype(BF16), preferred_element_type=F32)
            o = inner + cross
            kd = (k.astype(F32) * k_decay).astype(BF16)
            kv = lax.dot_general(kd, v, (((0,), (0,)), ((), ())), preferred_element_type=F32)
            state_ref[hh] = chunk_decay * st + kv
            mu = jnp.mean(o, axis=-1, keepdims=True)
            oc = o - mu
            var = jnp.mean(oc * oc, axis=-1, keepdims=True)
            gt = g_ref[rows, cols].astype(F32)
            o_ref[rows, cols] = (oc * lax.rsqrt(var + EPS) * gn * (gt * _sigmoid(gt))).astype(BF16)


def _retention(rq, rk, rv, rg, gn):
    bsz, t, w = rq.shape
    tok = pl.BlockSpec((None, RET_TC, w), lambda b, i: (b, i, 0))
    return pl.pallas_call(
        _ret_kernel,
        out_shape=jax.ShapeDtypeStruct((bsz, t, w), BF16),
        grid=(bsz, t // RET_TC),
        in_specs=[tok, tok, tok, tok, pl.BlockSpec((1, w), lambda b, i: (0, 0))],
        out_specs=tok,
        scratch_shapes=[pltpu.VMEM((RET_HEADS, RET_HEAD_DIM, RET_HEAD_DIM), F32)],
        compiler_params=pltpu.CompilerParams(dimension_semantics=("arbitrary", "arbitrary")),
        name="retention",
    )(rq, rk, rv, rg, gn)


def _attn_kernel(q_ref, kp_ref, kc_ref, vp_ref, vc_ref, g_ref, o_ref, kk_ref, vv_ref, ob_ref, lb_ref):
    tt, blk = ATTN_TT, ATTN_BLOCK
    tile = pl.program_id(2)
    kk_ref[0:tt, :] = kp_ref[...]
    kk_ref[tt:2 * tt, :] = kc_ref[...]
    vv_ref[0:tt, :] = vp_ref[...]
    vv_ref[tt:2 * tt, :] = vc_ref[...]

    lane = lax.broadcasted_iota(jnp.int32, (blk, LANES), 1)
    lo = lane < ATTN_HEAD_DIM
    qi = lax.broadcasted_iota(jnp.int32, (blk, 2 * blk), 0)
    kj = lax.broadcasted_iota(jnp.int32, (blk, 2 * blk), 1)
    neg = jnp.float32(-jnp.inf)
    bias = jnp.where((kj >= qi) & (kj <= qi + blk), 0.0, neg)
    bias_first = jnp.where(kj >= blk, bias, neg)

    for bi, (window, r) in enumerate(DILATED_PATTERNS):
        assert window // r == blk
        nres = tt // (blk * r)

        def body(idx, carry, r=r, bi=bi, nres=nres):
            s = idx % r
            j = idx // r
            qstart = s + r * blk * j
            kstart = tt + s + r * blk * (j - 1)
            q2 = q_ref[pl.ds(qstart, blk, stride=r), :]
            k2 = kk_ref[pl.ds(kstart, 2 * blk, stride=r), :].astype(BF16)
            v2 = vv_ref[pl.ds(kstart, 2 * blk, stride=r), :].astype(BF16)
            first = jnp.logical_and(tile == 0, j == 0)
            b = jnp.where(first, bias_first, bias)
            outs, lses = [], []
            for hh in range(2):
                qm = jnp.where(lo if hh == 0 else jnp.logical_not(lo), q2, 0.0).astype(BF16)
                sc = lax.dot_general(qm, k2, (((1,), (1,)), ((), ())), preferred_element_type=F32) + b
                m = jnp.max(sc, axis=-1, keepdims=True)
                p = jnp.exp(sc - m)
                l = jnp.sum(p, axis=-1, keepdims=True)
                o = jnp.dot(p.astype(BF16), v2, preferred_element_type=F32)
                outs.append(o / l)
                lses.append(jnp.broadcast_to(m + jnp.log(l), (blk, LANES)))
            ob_ref[bi, pl.ds(qstart, blk, stride=r), :] = jnp.where(lo, outs[0], outs[1])
            lb_ref[bi, pl.ds(qstart, blk, stride=r), :] = jnp.where(lo, lses[0], lses[1])
            return carry

        lax.fori_loop(0, tt // blk, body, 0)

    rc = 256
    lane_c = lax.broadcasted_iota(jnp.int32, (rc, LANES), 1)
    lo_c = lane_c < ATTN_HEAD_DIM
    gain = g_ref[...]

    def combine(ci, carry):
        rows = pl.ds(pl.multiple_of(ci * rc, rc), rc)
        l0, l1, l2 = lb_ref[0, rows, :], lb_ref[1, rows, :], lb_ref[2, rows, :]
        mx = jnp.maximum(jnp.maximum(l0, l1), l2)
        e0, e1, e2 = jnp.exp(l0 - mx), jnp.exp(l1 - mx), jnp.exp(l2 - mx)
        att = (e0 * ob_ref[0, rows, :] + e1 * ob_ref[1, rows, :] + e2 * ob_ref[2, rows, :]) / (e0 + e1 + e2)
        sq = att * att
        ms_lo = jnp.sum(jnp.where(lo_c, sq, 0.0), axis=-1, keepdims=True)
        ms_hi = jnp.sum(jnp.where(lo_c, 0.0, sq), axis=-1, keepdims=True)
        ms = jnp.where(lo_c, ms_lo, ms_hi) * (1.0 / ATTN_HEAD_DIM)
        o_ref[rows, :] = (att * lax.rsqrt(ms + EPS) * gain).astype(BF16)
        return carry

    lax.fori_loop(0, tt // rc, combine, 0)


def _dilated_attention(aq, ak, av, gain):
    bsz, t, w = aq.shape
    tt = ATTN_TT
    ngrp = w // LANES
    cur = lambda b, hp, i: (b, i, hp)
    prev = lambda b, hp, i: (b, jnp.maximum(i - 1, 0), hp)
    blk_cur = pl.BlockSpec((None, tt, LANES), cur)
    blk_prev = pl.BlockSpec((None, tt, LANES), prev)
    return pl.pallas_call(
        _attn_kernel,
        out_shape=jax.ShapeDtypeStruct((bsz, t, w), BF16),
        grid=(bsz, ngrp, t // tt),
        in_specs=[blk_cur, blk_prev, blk_cur, blk_prev, blk_cur,
                  pl.BlockSpec((1, LANES), lambda b, hp, i: (0, hp))],
        out_specs=blk_cur,
        scratch_shapes=[
            pltpu.VMEM((2 * tt, LANES), F32), pltpu.VMEM((2 * tt, LANES), F32),
            pltpu.VMEM((3, tt, LANES), F32), pltpu.VMEM((3, tt, LANES), F32),
        ],
        compiler_params=pltpu.CompilerParams(
            dimension_semantics=("arbitrary", "arbitrary", "arbitrary"), vmem_limit_bytes=VMEM_LIMIT),
        name="dilated_attention",
    )(aq, ak, ak, av, av, gain)


def _out_proj_kernel(x_ref, mod_ref, ret_ref, att_ref, w_ref, o_ref):
    y = jnp.dot(ret_ref[...], w_ref[0:RET_WIDTH, :], preferred_element_type=F32)
    y = y + jnp.dot(att_ref[...], w_ref[RET_WIDTH:, :], preferred_element_type=F32)
    o_ref[...] = x_ref[...] + mod_ref[5:6, :] * y


def _mixer_out_proj(x, mod, ret, att, w):
    bsz, t, d = x.shape
    tm = PROJ_TM
    tok = lambda b, i: (b, i, 0)
    return pl.pallas_call(
        _out_proj_kernel,
        out_shape=jax.ShapeDtypeStruct((bsz, t, d), F32),
        grid=(bsz, t // tm),
        in_specs=[
            pl.BlockSpec((None, tm, d), tok),
            pl.BlockSpec((None, N_MOD, d), lambda b, i: (b, 0, 0)),
            pl.BlockSpec((None, tm, RET_WIDTH), tok),
            pl.BlockSpec((None, tm, ATTN_WIDTH), tok),
            pl.BlockSpec((d, d), lambda b, i: (0, 0), pipeline_mode=pl.Buffered(1)),
        ],
        out_specs=pl.BlockSpec((None, tm, d), tok),
        compiler_params=pltpu.CompilerParams(
            dimension_semantics=("arbitrary", "arbitrary"), vmem_limit_bytes=VMEM_LIMIT),
        name="mixer_out_proj",
    )(x, mod, ret, att, w)


def kernel(x, c, w_ada, b_ada, norm1_g, ffn1_w_in, ffn1_w_out, norm_mix_g, w_in_mix, ret_gn_g, attn_norm_g,
           w_out_mix, norm2_g, ffn2_w_in, ffn2_w_out, norm_f_g):
    depth = w_ada.shape[0]
    assert depth >= 1, "the final RMSNorm is fused into the last layer's second FFN"
    bsz, t, d = x.shape
    gf = norm_f_g.reshape(1, d)
    for l in range(depth):
        last = l == depth - 1
        mod = _adaln_mod(c, w_ada[l], b_ada[l]).reshape(bsz, N_MOD, d)
        x = _ffn(x, mod, norm1_g[l].reshape(1, d), ffn1_w_in[l].astype(BF16), ffn1_w_out[l].astype(BF16), gf,
                 sub=0, final_norm=False)
        rq, rk, rv, rg, aq, ak, av = _mixer_in_proj(x, mod, norm_mix_g[l].reshape(1, d), w_in_mix[l].astype(BF16))
        ret = _retention(rq, rk, rv, rg, ret_gn_g[l].reshape(1, RET_WIDTH))
        att = _dilated_attention(aq, ak, av, attn_norm_g[l].reshape(1, ATTN_WIDTH))
        x = _mixer_out_proj(x, mod, ret, att, w_out_mix[l].astype(BF16))
        x = _ffn(x, mod, norm2_g[l].reshape(1, d), ffn2_w_in[l].astype(BF16), ffn2_w_out[l].astype(BF16), gf,
                 sub=2, final_norm=last)
    return x
```

```python
import functools
import math

import jax
import jax.numpy as jnp
from jax import lax
from jax.experimental import pallas as pl
from jax.experimental.pallas import tpu as pltpu

D_MODEL = 1024
RET_WIDTH = 512
RET_HEAD_DIM = 128
RET_HEADS = 4
RET_CHUNK = 128
ATTN_WIDTH = 512
ATTN_HEAD_DIM = 64
DILATED_PATTERNS = ((128, 1), (512, 4), (2048, 16))
ATTN_BLOCK = 128
ROPE_THETA = 10000.0
D_FF = 2816
N_MOD = 9
IN_COLS = 4 * RET_WIDTH + 3 * ATTN_WIDTH
EPS = 1e-6
LOG2_E = math.log2(math.e)

LANES = 128
VMEM_LIMIT = 56 * 1024 * 1024

FFN_TM = 512
FFN_FC = 256
PROJ_TM = 512
RET_TC = 512
ATTN_TT = ATTN_BLOCK * DILATED_PATTERNS[-1][1]
BF16 = jnp.bfloat16
F32 = jnp.float32


def _sigmoid(x):
    return 1.0 / (1.0 + jnp.exp(-x))


def _norm_mod(x, g, shift, scale):
    ms = jnp.mean(x * x, axis=-1, keepdims=True)
    return (x * lax.rsqrt(ms + EPS) * g) * (1.0 + scale) + shift


def _mod_kernel(c_ref, w_ref, b_ref, o_ref):
    c = c_ref[...]
    ca = c * _sigmoid(c)
    o_ref[...] = jnp.dot(ca, w_ref[...], preferred_element_type=F32) + b_ref[...]


def _adaln_mod(c, w, b):
    bsz, d = c.shape
    n = w.shape[1]
    tn = 1024
    return pl.pallas_call(
        _mod_kernel,
        out_shape=jax.ShapeDtypeStruct((bsz, n), F32),
        grid=(n // tn,),
        in_specs=[
            pl.BlockSpec((bsz, d), lambda j: (0, 0)),
            pl.BlockSpec((d, tn), lambda j: (0, j)),
            pl.BlockSpec((1, tn), lambda j: (0, j)),
        ],
        out_specs=pl.BlockSpec((bsz, tn), lambda j: (0, j)),
        compiler_params=pltpu.CompilerParams(dimension_semantics=("arbitrary",)),
        name="adaln_mod",
    )(c, w, b.reshape(1, n))


def _ffn_kernel(x_ref, mod_ref, g_ref, win_ref, wout_ref, gf_ref, o_ref, h_ref, act_ref, *, sub, final_norm):
    x = x_ref[...]
    shift = mod_ref[3 * sub:3 * sub + 1, :]
    scale = mod_ref[3 * sub + 1:3 * sub + 2, :]
    gate = mod_ref[3 * sub + 2:3 * sub + 3, :]
    h_ref[...] = _norm_mod(x, g_ref[...], shift, scale).astype(BF16)
    for j in range(D_FF // FFN_FC):
        h = h_ref[...]
        a = jnp.dot(h, win_ref[:, j * FFN_FC:(j + 1) * FFN_FC], preferred_element_type=F32)
        b = jnp.dot(h, win_ref[:, D_FF + j * FFN_FC:D_FF + (j + 1) * FFN_FC], preferred_element_type=F32)
        act_ref[:, j * FFN_FC:(j + 1) * FFN_FC] = (a * _sigmoid(a) * b).astype(BF16)
    y = jnp.dot(act_ref[...], wout_ref[...], preferred_element_type=F32)
    xn = x + 0.5 * gate * y
    if final_norm:
        ms = jnp.mean(xn * xn, axis=-1, keepdims=True)
        xn = xn * lax.rsqrt(ms + EPS) * gf_ref[...]
    o_ref[...] = xn


def _ffn(x, mod, g, w_in, w_out, g_final, *, sub, final_norm):
    bsz, t, d = x.shape
    tm = FFN_TM
    kern = functools.partial(_ffn_kernel, sub=sub, final_norm=final_norm)
    return pl.pallas_call(
        kern,
        out_shape=jax.ShapeDtypeStruct((bsz, t, d), F32),
        grid=(bsz, t // tm),
        in_specs=[
            pl.BlockSpec((None, tm, d), lambda b, i: (b, i, 0)),
            pl.BlockSpec((None, N_MOD, d), lambda b, i: (b, 0, 0)),
            pl.BlockSpec((1, d), lambda b, i: (0, 0)),
            pl.BlockSpec((d, 2 * D_FF), lambda b, i: (0, 0), pipeline_mode=pl.Buffered(1)),
            pl.BlockSpec((D_FF, d), lambda b, i: (0, 0), pipeline_mode=pl.Buffered(1)),
            pl.BlockSpec((1, d), lambda b, i: (0, 0)),
        ],
        out_specs=pl.BlockSpec((None, tm, d), lambda b, i: (b, i, 0)),
        scratch_shapes=[pltpu.VMEM((tm, d), BF16), pltpu.VMEM((tm, D_FF), BF16)],
        compiler_params=pltpu.CompilerParams(
            dimension_semantics=("arbitrary", "arbitrary"), vmem_limit_bytes=VMEM_LIMIT),
        name=f"ffn{sub // 2 + 1}",
    )(x, mod, g, w_in, w_out, g_final)


def _store_residue_layouts(val, nat_ref, p4_ref, p16_ref, t1_ref, t4_ref):
    tm = val.shape[0]
    nat_ref[...] = val.astype(BF16)
    for g in range(val.shape[1] // LANES):
        cols = slice(g * LANES, (g + 1) * LANES)
        t1_ref[g] = val[:, cols]
        for s4 in range(4):
            y = t1_ref[g, pl.ds(s4, tm // 4, stride=4), :]
            p4_ref[s4, :, cols] = y.astype(BF16)
            t4_ref[g, s4 * (tm // 4):(s4 + 1) * (tm // 4), :] = y
        for s16 in range(16):
            s4, u = s16 % 4, s16 // 4
            y = t4_ref[g, pl.ds(s4 * (tm // 4) + u, tm // 16, stride=4), :]
            p16_ref[s16, :, cols] = y.astype(BF16)


def _proj_kernel(x_ref, mod_ref, g_ref, w_ref, cr_ref, sr_ref, ca_ref, sa_ref, sb_ref,
                 rq_ref, rk_ref, rv_ref, rg_ref,
                 aq1_ref, aq4_ref, aq16_ref, ak1_ref, ak4_ref, ak16_ref, av1_ref, av4_ref, av16_ref,
                 h_ref, t1_ref, t4_ref):
    x = x_ref[...]
    h_ref[...] = _norm_mod(x, g_ref[...], mod_ref[3:4, :], mod_ref[4:5, :]).astype(BF16)

    def proj(c):
        return jnp.dot(h_ref[...], w_ref[:, c * 512:(c + 1) * 512], preferred_element_type=F32)

    def rope_ret(p):
        cos, sin = cr_ref[...], sr_ref[...]
        outs = []
        for hh in range(RET_HEADS):
            xh = p[:, hh * LANES:(hh + 1) * LANES]
            outs.append(xh * cos + pltpu.roll(xh, 64, 1) * sin)
        return jnp.concatenate(outs, axis=1)

    def rope_attn(p):
        cos, sin_lo, sin_hi = ca_ref[...], sa_ref[...], sb_ref[...]
        outs = []
        for hh in range(ATTN_WIDTH // LANES):
            xh = p[:, hh * LANES:(hh + 1) * LANES]
            outs.append(xh * cos + pltpu.roll(xh, 96, 1) * sin_lo + pltpu.roll(xh, 32, 1) * sin_hi)
        return jnp.concatenate(outs, axis=1)

    rq_ref[...] = rope_ret(proj(0)).astype(BF16)
    rk_ref[...] = (rope_ret(proj(1)) * (RET_HEAD_DIM ** -0.5)).astype(BF16)
    rv_ref[...] = proj(2).astype(BF16)
    rg_ref[...] = proj(3).astype(BF16)
    _store_residue_layouts(rope_attn(proj(4)) * (LOG2_E * ATTN_HEAD_DIM ** -0.5),
                           aq1_ref, aq4_ref, aq16_ref, t1_ref, t4_ref)
    _store_residue_layouts(rope_attn(proj(5)), ak1_ref, ak4_ref, ak16_ref, t1_ref, t4_ref)
    _store_residue_layouts(proj(6), av1_ref, av4_ref, av16_ref, t1_ref, t4_ref)


def _rope_tables(t):
    pos = jnp.arange(t, dtype=F32)[:, None]
    inv_r = ROPE_THETA ** (-jnp.arange(0, RET_HEAD_DIM, 2, dtype=F32) / RET_HEAD_DIM)
    ang_r = pos * inv_r[None, :]
    cos_r = jnp.concatenate([jnp.cos(ang_r)] * 2, axis=1)
    sin_r = jnp.concatenate([-jnp.sin(ang_r), jnp.sin(ang_r)], axis=1)
    inv_a = ROPE_THETA ** (-jnp.arange(0, ATTN_HEAD_DIM, 2, dtype=F32) / ATTN_HEAD_DIM)
    ang_a = pos * inv_a[None, :]
    ca, sa = jnp.cos(ang_a), jnp.sin(ang_a)
    z = jnp.zeros_like(sa)
    cos_a = jnp.concatenate([ca] * 4, axis=1)
    sin_lo = jnp.concatenate([-sa, z, -sa, z], axis=1)
    sin_hi = jnp.concatenate([z, sa, z, sa], axis=1)
    return cos_r, sin_r, cos_a, sin_lo, sin_hi


def _mixer_in_proj(x, mod, g, w):
    bsz, t, d = x.shape
    tm = PROJ_TM
    tables = _rope_tables(t)
    tok = lambda b, i: (b, i, 0)
    tab = pl.BlockSpec((tm, LANES), lambda b, i: (i, 0))
    aw = ATTN_WIDTH
    nat = jax.ShapeDtypeStruct((bsz, t, aw), BF16)
    p4 = jax.ShapeDtypeStruct((bsz, 4, t // 4, aw), BF16)
    p16 = jax.ShapeDtypeStruct((bsz, 16, t // 16, aw), BF16)
    nat_spec = pl.BlockSpec((None, tm, aw), tok)
    p4_spec = pl.BlockSpec((None, 4, tm // 4, aw), lambda b, i: (b, 0, i, 0))
    p16_spec = pl.BlockSpec((None, 16, tm // 16, aw), lambda b, i: (b, 0, i, 0))
    return pl.pallas_call(
        _proj_kernel,
        out_shape=[nat] * 4 + [nat, p4, p16] * 3,
        grid=(bsz, t // tm),
        in_specs=[
            pl.BlockSpec((None, tm, d), tok),
            pl.BlockSpec((None, N_MOD, d), lambda b, i: (b, 0, 0)),
            pl.BlockSpec((1, d), lambda b, i: (0, 0)),
            pl.BlockSpec((d, IN_COLS), lambda b, i: (0, 0), pipeline_mode=pl.Buffered(1)),
            tab, tab, tab, tab, tab,
        ],
        out_specs=[nat_spec] * 4 + [nat_spec, p4_spec, p16_spec] * 3,
        scratch_shapes=[pltpu.VMEM((tm, d), BF16),
                        pltpu.VMEM((aw // LANES, tm, LANES), F32), pltpu.VMEM((aw // LANES, tm, LANES), F32)],
        compiler_params=pltpu.CompilerParams(
            dimension_semantics=("arbitrary", "arbitrary"), vmem_limit_bytes=VMEM_LIMIT),
        name="mixer_in_proj",
    )(x, mod, g, w, *tables)


def _ret_kernel(q_ref, k_ref, v_ref, g_ref, gn_ref, o_ref, state_ref):
    c = RET_CHUNK

    @pl.when(pl.program_id(1) == 0)
    def _():
        state_ref[...] = jnp.zeros_like(state_ref)

    ii = lax.broadcasted_iota(jnp.int32, (c, c), 0)
    jj = lax.broadcasted_iota(jnp.int32, (c, c), 1)
    diff = (ii - jj).astype(F32)
    ivec = lax.broadcasted_iota(jnp.int32, (c, 1), 0).astype(F32)
    for hh in range(RET_HEADS):
        log_g = math.log1p(-(2.0 ** (-5.0 - hh)))
        decay = jnp.where(diff >= 0, jnp.exp(jnp.maximum(diff, 0.0) * log_g), 0.0)
        k_decay = jnp.exp((c - 1.0 - ivec) * log_g)
        q_decay = jnp.exp((ivec + 1.0) * log_g)
        chunk_decay = math.exp(c * log_g)
        cols = slice(hh * LANES, (hh + 1) * LANES)
        gn = gn_ref[:, cols]
        for n in range(RET_TC // c):
            rows = slice(n * c, (n + 1) * c)
            q = q_ref[rows, cols]
            k = k_ref[rows, cols]
            v = v_ref[rows, cols]
            s = lax.dot_general(q, k, (((1,), (1,)), ((), ())), preferred_element_type=F32) * decay
            inner = jnp.dot(s.astype(BF16), v, preferred_element_type=F32)
            st = state_ref[hh]
            qd = (q.astype(F32) * q_decay).astype(BF16)
            cross = jnp.dot(qd, st.astype(BF16), preferred_element_type=F32)
            o = inner + cross
            kd = (k.astype(F32) * k_decay).astype(BF16)
            kv = lax.dot_general(kd, v, (((0,), (0,)), ((), ())), preferred_element_type=F32)
            state_ref[hh] = chunk_decay * st + kv
            mu = jnp.mean(o, axis=-1, keepdims=True)
            oc = o - mu
            var = jnp.mean(oc * oc, axis=-1, keepdims=True)
            gt = g_ref[rows, cols].astype(F32)
            o_ref[rows, cols] = (oc * lax.rsqrt(var + EPS) * gn * (gt * _sigmoid(gt))).astype(BF16)


def _retention(rq, rk, rv, rg, gn):
    bsz, t, w = rq.shape
    tok = pl.BlockSpec((None, RET_TC, w), lambda b, i: (b, i, 0))
    return pl.pallas_call(
        _ret_kernel,
        out_shape=jax.ShapeDtypeStruct((bsz, t, w), BF16),
        grid=(bsz, t // RET_TC),
        in_specs=[tok, tok, tok, tok, pl.BlockSpec((1, w), lambda b, i: (0, 0))],
        out_specs=tok,
        scratch_shapes=[pltpu.VMEM((RET_HEADS, RET_HEAD_DIM, RET_HEAD_DIM), F32)],
        compiler_params=pltpu.CompilerParams(dimension_semantics=("arbitrary", "arbitrary")),
        name="retention",
    )(rq, rk, rv, rg, gn)


def _attn_unit(q2, k2, v2, bias, lo):
    blk = ATTN_BLOCK
    zero = jnp.zeros_like(q2)
    qs = jnp.concatenate([jnp.where(lo, q2, zero), jnp.where(lo, zero, q2)], axis=0)
    sc = lax.dot_general(qs, k2, (((1,), (1,)), ((), ())), preferred_element_type=F32)
    sc = sc + jnp.concatenate([bias, bias], axis=0)
    m = jnp.max(sc, axis=-1, keepdims=True)
    p = jnp.exp2(sc - m).astype(BF16)
    va = jnp.concatenate([v2, jnp.ones_like(v2)], axis=1)
    oa = jnp.dot(p, va, preferred_element_type=F32)
    o = jnp.where(lo, oa[:blk, :LANES], oa[blk:, :LANES])
    l = jnp.where(lo, oa[:blk, LANES:], oa[blk:, LANES:])
    mm = jnp.where(lo, jnp.broadcast_to(m[:blk], (blk, LANES)), jnp.broadcast_to(m[blk:], (blk, LANES)))
    return o / l, mm + jnp.log2(l)


def _attn_kernel(q1_ref, q4_ref, q16_ref,
                 k1p_ref, k1c_ref, k4p_ref, k4c_ref, k16p_ref, k16c_ref,
                 v1p_ref, v1c_ref, v4p_ref, v4c_ref, v16p_ref, v16c_ref,
                 g_ref, o_ref, bias_ref, ob_ref, lb_ref):
    tt, blk = ATTN_TT, ATTN_BLOCK
    tile = pl.program_id(2)
    lane = lax.broadcasted_iota(jnp.int32, (blk, LANES), 1)
    lo = lane < ATTN_HEAD_DIM
    qi = lax.broadcasted_iota(jnp.int32, (blk, 2 * blk), 0)
    kj = lax.broadcasted_iota(jnp.int32, (blk, 2 * blk), 1)
    neg = jnp.float32(-jnp.inf)
    band = jnp.where((kj >= qi) & (kj <= qi + blk), 0.0, neg)
    bias_ref[0] = band
    bias_ref[1] = jnp.where(jnp.logical_or(kj >= blk, tile > 0), band, neg)

    q_refs = (q1_ref, q4_ref, q16_ref)
    k_refs = ((k1p_ref, k1c_ref), (k4p_ref, k4c_ref), (k16p_ref, k16c_ref))
    v_refs = ((v1p_ref, v1c_ref), (v4p_ref, v4c_ref), (v16p_ref, v16c_ref))

    def rows_of(ref, r, s, lo_row, n):
        return ref[lo_row:lo_row + n, :] if r == 1 else ref[s, lo_row:lo_row + n, :]

    def keys(refs, r, s, j):
        prev_ref, cur_ref = refs
        if j == 0:
            return jnp.concatenate([rows_of(prev_ref, r, s, 0, blk), rows_of(cur_ref, r, s, 0, blk)], axis=0)
        return rows_of(cur_ref, r, s, (j - 1) * blk, 2 * blk)

    for bi, (window, r) in enumerate(DILATED_PATTERNS):
        assert window // r == blk
        for s in range(r):
            for j in range(tt // (blk * r)):
                q2 = rows_of(q_refs[bi], r, s, j * blk, blk)
                k2 = keys(k_refs[bi], r, s, j)
                v2 = keys(v_refs[bi], r, s, j)
                o, lse = _attn_unit(q2, k2, v2, bias_ref[1 if j == 0 else 0], lo)
                start = s + r * blk * j
                rows = pl.ds(start, blk, stride=r) if r > 1 else pl.ds(start, blk)
                ob_ref[bi, rows, :] = o
                lb_ref[bi, rows, :] = lse

    rc = 256
    lane_c = lax.broadcasted_iota(jnp.int32, (rc, LANES), 1)
    lo_c = lane_c < ATTN_HEAD_DIM
    gain = g_ref[...]

    for ci in range(tt // rc):
        rows = slice(ci * rc, (ci + 1) * rc)
        l0, l1, l2 = lb_ref[0, rows, :], lb_ref[1, rows, :], lb_ref[2, rows, :]
        mx = jnp.maximum(jnp.maximum(l0, l1), l2)
        e0, e1, e2 = jnp.exp2(l0 - mx), jnp.exp2(l1 - mx), jnp.exp2(l2 - mx)
        att = (e0 * ob_ref[0, rows, :] + e1 * ob_ref[1, rows, :] + e2 * ob_ref[2, rows, :]) / (e0 + e1 + e2)
        sq = att * att
        ms_lo = jnp.sum(jnp.where(lo_c, sq, 0.0), axis=-1, keepdims=True)
        ms_hi = jnp.sum(jnp.where(lo_c, 0.0, sq), axis=-1, keepdims=True)
        ms = jnp.where(lo_c, ms_lo, ms_hi) * (1.0 / ATTN_HEAD_DIM)
        o_ref[rows, :] = (att * lax.rsqrt(ms + EPS) * gain).astype(BF16)


def _dilated_attention(aq, ak, av, gain):
    bsz, t, w = aq[0].shape
    tt, blk = ATTN_TT, ATTN_BLOCK
    ngrp = w // LANES

    def cur_prev(r):
        rows = tt // r
        if r == 1:
            cur = pl.BlockSpec((None, rows, LANES), lambda b, hp, i: (b, i, hp))
            prev = pl.BlockSpec((None, blk, LANES),
                                lambda b, hp, i: (b, jnp.maximum(i * (rows // blk) - 1, 0), hp))
        else:
            cur = pl.BlockSpec((None, r, rows, LANES), lambda b, hp, i: (b, 0, i, hp))
            prev = pl.BlockSpec((None, r, blk, LANES),
                                lambda b, hp, i: (b, 0, jnp.maximum(i * (rows // blk) - 1, 0), hp))
        return cur, prev

    specs = [cur_prev(r) for _, r in DILATED_PATTERNS]
    q_specs = [c for c, _ in specs]
    kv_specs = [sp for c, p in specs for sp in (p, c)]
    kv_args = lambda a: [x for arr in a for x in (arr, arr)]
    return pl.pallas_call(
        _attn_kernel,
        out_shape=jax.ShapeDtypeStruct((bsz, t, w), BF16),
        grid=(bsz, ngrp, t // tt),
        in_specs=q_specs + kv_specs + kv_specs + [pl.BlockSpec((1, LANES), lambda b, hp, i: (0, hp))],
        out_specs=pl.BlockSpec((None, tt, LANES), lambda b, hp, i: (b, i, hp)),
        scratch_shapes=[
            pltpu.VMEM((2, blk, 2 * blk), F32),
            pltpu.VMEM((3, tt, LANES), F32), pltpu.VMEM((3, tt, LANES), F32),
        ],
        compiler_params=pltpu.CompilerParams(
            dimension_semantics=("arbitrary", "arbitrary", "arbitrary"), vmem_limit_bytes=VMEM_LIMIT),
        name="dilated_attention",
    )(*aq, *kv_args(ak), *kv_args(av), gain)


def _out_proj_kernel(x_ref, mod_ref, ret_ref, att_ref, w_ref, o_ref):
    y = jnp.dot(ret_ref[...], w_ref[0:RET_WIDTH, :], preferred_element_type=F32)
    y = y + jnp.dot(att_ref[...], w_ref[RET_WIDTH:, :], preferred_element_type=F32)
    o_ref[...] = x_ref[...] + mod_ref[5:6, :] * y


def _mixer_out_proj(x, mod, ret, att, w):
    bsz, t, d = x.shape
    tm = PROJ_TM
    tok = lambda b, i: (b, i, 0)
    return pl.pallas_call(
        _out_proj_kernel,
        out_shape=jax.ShapeDtypeStruct((bsz, t, d), F32),
        grid=(bsz, t // tm),
        in_specs=[
            pl.BlockSpec((None, tm, d), tok),
            pl.BlockSpec((None, N_MOD, d), lambda b, i: (b, 0, 0)),
            pl.BlockSpec((None, tm, RET_WIDTH), tok),
            pl.BlockSpec((None, tm, ATTN_WIDTH), tok),
            pl.BlockSpec((d, d), lambda b, i: (0, 0), pipeline_mode=pl.Buffered(1)),
        ],
        out_specs=pl.BlockSpec((None, tm, d), tok),
        compiler_params=pltpu.CompilerParams(
            dimension_semantics=("arbitrary", "arbitrary"), vmem_limit_bytes=VMEM_LIMIT),
        name="mixer_out_proj",
    )(x, mod, ret, att, w)


def kernel(x, c, w_ada, b_ada, norm1_g, ffn1_w_in, ffn1_w_out, norm_mix_g, w_in_mix, ret_gn_g, attn_norm_g,
           w_out_mix, norm2_g, ffn2_w_in, ffn2_w_out, norm_f_g):
    depth = w_ada.shape[0]
    assert depth >= 1, "the final RMSNorm is fused into the last layer's second FFN"
    bsz, t, d = x.shape
    gf = norm_f_g.reshape(1, d)
    for l in range(depth):
        last = l == depth - 1
        mod = _adaln_mod(c, w_ada[l], b_ada[l]).reshape(bsz, N_MOD, d)
        x = _ffn(x, mod, norm1_g[l].reshape(1, d), ffn1_w_in[l].astype(BF16), ffn1_w_out[l].astype(BF16), gf,
                 sub=0, final_norm=False)
        proj = _mixer_in_proj(x, mod, norm_mix_g[l].reshape(1, d), w_in_mix[l].astype(BF16))
        rq, rk, rv, rg = proj[:4]
        ret = _retention(rq, rk, rv, rg, ret_gn_g[l].reshape(1, RET_WIDTH))
        att = _dilated_attention(proj[4:7], proj[7:10], proj[10:13], attn_norm_g[l].reshape(1, ATTN_WIDTH))
        x = _mixer_out_proj(x, mod, ret, att, w_out_mix[l].astype(BF16))
        x = _ffn(x, mod, norm2_g[l].reshape(1, d), ffn2_w_in[l].astype(BF16), ffn2_w_out[l].astype(BF16), gf,
                 sub=2, final_norm=last)
    return x
```

```python
import functools
import math

import jax
import jax.numpy as jnp
from jax import lax
from jax.experimental import pallas as pl
from jax.experimental.pallas import tpu as pltpu

D_MODEL = 1024
RET_WIDTH = 512
RET_HEAD_DIM = 128
RET_HEADS = 4
RET_CHUNK = 128
ATTN_WIDTH = 512
ATTN_HEAD_DIM = 64
DILATED_PATTERNS = ((128, 1), (512, 4), (2048, 16))
ATTN_BLOCK = 128
ROPE_THETA = 10000.0
D_FF = 2816
N_MOD = 9
IN_COLS = 4 * RET_WIDTH + 3 * ATTN_WIDTH
EPS = 1e-6
LOG2_E = math.log2(math.e)

LANES = 128
VMEM_LIMIT = 56 * 1024 * 1024

FFN_TM = 512
FFN_FC = 256
PROJ_TM = 512
RET_TC = 1024
RET_STEP = 256
ATTN_TT = ATTN_BLOCK * DILATED_PATTERNS[-1][1]
BF16 = jnp.bfloat16
F32 = jnp.float32


def _sigmoid(x):
    return 1.0 / (1.0 + jnp.exp(-x))


def _norm_mod(x, g, shift, scale):
    ms = jnp.mean(x * x, axis=-1, keepdims=True)
    return (x * lax.rsqrt(ms + EPS) * g) * (1.0 + scale) + shift


def _mod_kernel(c_ref, w_ref, b_ref, o_ref):
    c = c_ref[...]
    ca = c * _sigmoid(c)
    o_ref[...] = jnp.dot(ca, w_ref[...], preferred_element_type=F32) + b_ref[...]


def _adaln_mod(c, w, b):
    bsz, d = c.shape
    n = w.shape[1]
    tn = 1024
    return pl.pallas_call(
        _mod_kernel,
        out_shape=jax.ShapeDtypeStruct((bsz, n), F32),
        grid=(n // tn,),
        in_specs=[
            pl.BlockSpec((bsz, d), lambda j: (0, 0)),
            pl.BlockSpec((d, tn), lambda j: (0, j)),
            pl.BlockSpec((1, tn), lambda j: (0, j)),
        ],
        out_specs=pl.BlockSpec((bsz, tn), lambda j: (0, j)),
        compiler_params=pltpu.CompilerParams(dimension_semantics=("arbitrary",)),
        name="adaln_mod",
    )(c, w, b.reshape(1, n))


def _ffn_kernel(*refs, sub, mixer_out, final_norm):
    if mixer_out:
        x_ref, ret_ref, att_ref, wmix_ref, *refs = refs
    else:
        x_ref, *refs = refs
    mod_ref, g_ref, win_ref, wout_ref, gf_ref, o_ref, h_ref, act_ref = refs
    x = x_ref[...]
    if mixer_out:
        y = jnp.dot(ret_ref[...], wmix_ref[0:RET_WIDTH, :], preferred_element_type=F32)
        y = y + jnp.dot(att_ref[...], wmix_ref[RET_WIDTH:, :], preferred_element_type=F32)
        x = x + mod_ref[5:6, :] * y
    shift = mod_ref[3 * sub:3 * sub + 1, :]
    scale = mod_ref[3 * sub + 1:3 * sub + 2, :]
    gate = mod_ref[3 * sub + 2:3 * sub + 3, :]
    h_ref[...] = _norm_mod(x, g_ref[...], shift, scale).astype(BF16)
    for j in range(D_FF // FFN_FC):
        h = h_ref[...]
        a = jnp.dot(h, win_ref[:, j * FFN_FC:(j + 1) * FFN_FC], preferred_element_type=F32)
        b = jnp.dot(h, win_ref[:, D_FF + j * FFN_FC:D_FF + (j + 1) * FFN_FC], preferred_element_type=F32)
        act_ref[:, j * FFN_FC:(j + 1) * FFN_FC] = (a * _sigmoid(a) * b).astype(BF16)
    y = jnp.dot(act_ref[...], wout_ref[...], preferred_element_type=F32)
    xn = x + 0.5 * gate * y
    if final_norm:
        ms = jnp.mean(xn * xn, axis=-1, keepdims=True)
        xn = xn * lax.rsqrt(ms + EPS) * gf_ref[...]
    o_ref[...] = xn


def _ffn(x, mod, g, w_in, w_out, g_final, *, sub, final_norm, mixer=None):
    bsz, t, d = x.shape
    tm = FFN_TM
    kern = functools.partial(_ffn_kernel, sub=sub, mixer_out=mixer is not None, final_norm=final_norm)
    tok = lambda b, i: (b, i, 0)
    mixer_specs, mixer_args = [], []
    if mixer is not None:
        ret, att, w_mix = mixer
        mixer_specs = [pl.BlockSpec((None, tm, ret.shape[-1]), tok), pl.BlockSpec((None, tm, att.shape[-1]), tok),
                       pl.BlockSpec(w_mix.shape, lambda b, i: (0, 0), pipeline_mode=pl.Buffered(1))]
        mixer_args = [ret, att, w_mix]
    return pl.pallas_call(
        kern,
        out_shape=jax.ShapeDtypeStruct((bsz, t, d), F32),
        grid=(bsz, t // tm),
        in_specs=[pl.BlockSpec((None, tm, d), tok)] + mixer_specs + [
            pl.BlockSpec((None, N_MOD, d), lambda b, i: (b, 0, 0)),
            pl.BlockSpec((1, d), lambda b, i: (0, 0)),
            pl.BlockSpec((d, 2 * D_FF), lambda b, i: (0, 0), pipeline_mode=pl.Buffered(1)),
            pl.BlockSpec((D_FF, d), lambda b, i: (0, 0), pipeline_mode=pl.Buffered(1)),
            pl.BlockSpec((1, d), lambda b, i: (0, 0)),
        ],
        out_specs=pl.BlockSpec((None, tm, d), lambda b, i: (b, i, 0)),
        scratch_shapes=[pltpu.VMEM((tm, d), BF16), pltpu.VMEM((tm, D_FF), BF16)],
        compiler_params=pltpu.CompilerParams(
            dimension_semantics=("arbitrary", "arbitrary"), vmem_limit_bytes=VMEM_LIMIT),
        name=f"ffn{sub // 2 + 1}",
    )(x, *mixer_args, mod, g, w_in, w_out, g_final)


def _store_residue_layouts(val, nat_ref, p4_ref, p16_ref, t1_ref, t4_ref):
    tm = val.shape[0]
    nat_ref[...] = val.astype(BF16)
    for g in range(val.shape[1] // LANES):
        cols = slice(g * LANES, (g + 1) * LANES)
        t1_ref[g] = val[:, cols]
        for s4 in range(4):
            y = t1_ref[g, pl.ds(s4, tm // 4, stride=4), :]
            p4_ref[s4, :, cols] = y.astype(BF16)
            t4_ref[g, s4 * (tm // 4):(s4 + 1) * (tm // 4), :] = y
        for s16 in range(16):
            s4, u = s16 % 4, s16 // 4
            y = t4_ref[g, pl.ds(s4 * (tm // 4) + u, tm // 16, stride=4), :]
            p16_ref[s16, :, cols] = y.astype(BF16)


def _proj_kernel(x_ref, mod_ref, g_ref, w_ref, tr_ref, br_ref, ta_ref, ba_ref,
                 rq_ref, rk_ref, rv_ref, rg_ref,
                 aq1_ref, aq4_ref, aq16_ref, ak1_ref, ak4_ref, ak16_ref, av1_ref, av4_ref, av16_ref,
                 h_ref, t1_ref, t4_ref):
    x = x_ref[...]
    h_ref[...] = _norm_mod(x, g_ref[...], mod_ref[3:4, :], mod_ref[4:5, :]).astype(BF16)

    def proj(c):
        return jnp.dot(h_ref[...], w_ref[:, c * 512:(c + 1) * 512], preferred_element_type=F32)

    def rope_ret(p, t_cos, t_sin):
        cb, sb, cbs, sbs = br_ref[0:1, :], br_ref[1:2, :], br_ref[2:3, :], br_ref[3:4, :]
        cos = t_cos * cb - t_sin * sb
        sin = t_cos * sbs + t_sin * cbs
        outs = []
        for hh in range(RET_HEADS):
            cols = slice(hh * LANES, (hh + 1) * LANES)
            xh = p[:, cols]
            outs.append(xh * cos[:, cols] + pltpu.roll(xh, RET_HEAD_DIM // 2, 1) * sin[:, cols])
        return jnp.concatenate(outs, axis=1)

    def rope_attn(p):
        t_cos, t_sin = ta_ref[0], ta_ref[1]
        cos = t_cos * ba_ref[0:1, :] - t_sin * ba_ref[1:2, :]
        sin_lo = t_cos * ba_ref[3:4, :] + t_sin * ba_ref[2:3, :]
        sin_hi = t_cos * ba_ref[5:6, :] + t_sin * ba_ref[4:5, :]
        outs = []
        for hh in range(ATTN_WIDTH // LANES):
            xh = p[:, hh * LANES:(hh + 1) * LANES]
            outs.append(xh * cos + pltpu.roll(xh, 96, 1) * sin_lo + pltpu.roll(xh, 32, 1) * sin_hi)
        return jnp.concatenate(outs, axis=1)

    rq_ref[...] = rope_ret(proj(0), tr_ref[0], tr_ref[1]).astype(BF16)
    rk_ref[...] = rope_ret(proj(1), tr_ref[2], tr_ref[3]).astype(BF16)
    rv_ref[...] = proj(2).astype(BF16)
    rg_ref[...] = proj(3).astype(BF16)
    _store_residue_layouts(rope_attn(proj(4)) * (LOG2_E * ATTN_HEAD_DIM ** -0.5),
                           aq1_ref, aq4_ref, aq16_ref, t1_ref, t4_ref)
    _store_residue_layouts(rope_attn(proj(5)), ak1_ref, ak4_ref, ak16_ref, t1_ref, t4_ref)
    _store_residue_layouts(proj(6), av1_ref, av4_ref, av16_ref, t1_ref, t4_ref)


def _rope_tables(t, tm):
    assert tm % RET_STEP == 0
    p = jnp.arange(tm, dtype=F32)[:, None]
    base = (jnp.arange(t // tm, dtype=F32) * tm)[:, None]
    ones = jnp.ones((RET_HEAD_DIM // 2,), F32)
    inv_r = ROPE_THETA ** (-jnp.arange(0, RET_HEAD_DIM, 2, dtype=F32) / RET_HEAD_DIM)
    f_r = jnp.concatenate([inv_r, inv_r])
    cos_p, sin_p = jnp.cos(p * f_r), jnp.sin(p * f_r)
    loc = p % RET_STEP
    log_g = [math.log1p(-(2.0 ** (-5.0 - hh))) for hh in range(RET_HEADS)]
    dq = [jnp.exp(loc * lg) for lg in log_g]
    dk = [jnp.exp(-loc * lg) * RET_HEAD_DIM ** -0.5 for lg in log_g]
    per_head = lambda scales, tab: jnp.concatenate([s * tab for s in scales], axis=1)
    tab_r = jnp.stack([per_head(dq, cos_p), per_head(dq, sin_p), per_head(dk, cos_p), per_head(dk, sin_p)])
    sign = jnp.concatenate([-ones, ones])
    cos_b, sin_b = jnp.cos(base * f_r), jnp.sin(base * f_r)
    heads = lambda a: jnp.concatenate([a] * RET_HEADS, axis=1)
    base_r = jnp.stack([heads(cos_b), heads(sin_b), heads(sign * cos_b), heads(sign * sin_b)], axis=1)
    inv_a = ROPE_THETA ** (-jnp.arange(0, ATTN_HEAD_DIM, 2, dtype=F32) / ATTN_HEAD_DIM)
    f_a = jnp.concatenate([inv_a] * 4)
    tab_a = jnp.stack([jnp.cos(p * f_a), jnp.sin(p * f_a)])
    half = jnp.ones((ATTN_HEAD_DIM // 2,), F32)
    pat_lo = jnp.concatenate([-half, 0 * half, -half, 0 * half])
    pat_hi = jnp.concatenate([0 * half, half, 0 * half, half])
    cos_ba, sin_ba = jnp.cos(base * f_a), jnp.sin(base * f_a)
    zero = jnp.zeros_like(cos_ba)
    base_a = jnp.stack([cos_ba, sin_ba, pat_lo * cos_ba, pat_lo * sin_ba, pat_hi * cos_ba, pat_hi * sin_ba,
                        zero, zero], axis=1)
    return tab_r, base_r, tab_a, base_a


def _mixer_in_proj(x, mod, g, w):
    bsz, t, d = x.shape
    tm = PROJ_TM
    tab_r, base_r, tab_a, base_a = _rope_tables(t, tm)
    tok = lambda b, i: (b, i, 0)
    const3 = lambda b, i: (0, 0, 0)
    aw = ATTN_WIDTH
    nat = jax.ShapeDtypeStruct((bsz, t, aw), BF16)
    p4 = jax.ShapeDtypeStruct((bsz, 4, t // 4, aw), BF16)
    p16 = jax.ShapeDtypeStruct((bsz, 16, t // 16, aw), BF16)
    nat_spec = pl.BlockSpec((None, tm, aw), tok)
    p4_spec = pl.BlockSpec((None, 4, tm // 4, aw), lambda b, i: (b, 0, i, 0))
    p16_spec = pl.BlockSpec((None, 16, tm // 16, aw), lambda b, i: (b, 0, i, 0))
    return pl.pallas_call(
        _proj_kernel,
        out_shape=[nat] * 4 + [nat, p4, p16] * 3,
        grid=(bsz, t // tm),
        in_specs=[
            pl.BlockSpec((None, tm, d), tok),
            pl.BlockSpec((None, N_MOD, d), lambda b, i: (b, 0, 0)),
            pl.BlockSpec((1, d), lambda b, i: (0, 0)),
            pl.BlockSpec((d, IN_COLS), lambda b, i: (0, 0), pipeline_mode=pl.Buffered(1)),
            pl.BlockSpec(tab_r.shape, const3, pipeline_mode=pl.Buffered(1)),
            pl.BlockSpec((None,) + base_r.shape[1:], lambda b, i: (i, 0, 0)),
            pl.BlockSpec(tab_a.shape, const3, pipeline_mode=pl.Buffered(1)),
            pl.BlockSpec((None,) + base_a.shape[1:], lambda b, i: (i, 0, 0)),
        ],
        out_specs=[nat_spec] * 4 + [nat_spec, p4_spec, p16_spec] * 3,
        scratch_shapes=[pltpu.VMEM((tm, d), BF16),
                        pltpu.VMEM((aw // LANES, tm, LANES), F32), pltpu.VMEM((aw // LANES, tm, LANES), F32)],
        compiler_params=pltpu.CompilerParams(
            dimension_semantics=("arbitrary", "arbitrary"), vmem_limit_bytes=VMEM_LIMIT),
        name="mixer_in_proj",
    )(x, mod, g, w, tab_r, base_r, tab_a, base_a)


def _block_diag(a, b):
    za, zb = jnp.zeros_like(a), jnp.zeros_like(b)
    return jnp.concatenate([jnp.concatenate([a, zb], axis=1), jnp.concatenate([za, b], axis=1)], axis=0)


def _ret_kernel(q_ref, k_ref, v_ref, g_ref, gn_ref, o_ref, state_ref, tri_ref):
    c, d = RET_STEP, RET_HEAD_DIM
    log_g = [math.log1p(-(2.0 ** (-5.0 - hh))) for hh in range(RET_HEADS)]

    @pl.when(jnp.logical_and(pl.program_id(0) == 0, pl.program_id(1) == 0))
    def _():
        ii = lax.broadcasted_iota(jnp.int32, (c, c), 0)
        jj = lax.broadcasted_iota(jnp.int32, (c, c), 1)
        tri_ref[...] = jnp.where(ii >= jj, 1.0, 0.0)

    @pl.when(pl.program_id(1) == 0)
    def _():
        state_ref[...] = jnp.zeros_like(state_ref)

    pairs = RET_HEADS // 2
    steps = RET_TC // c
    contract_rows = (((0,), (0,)), ((), ()))
    kv = {}
    for hp in range(pairs):
        cols = slice(2 * hp * d, 2 * (hp + 1) * d)
        for n in range(steps):
            rows = slice(n * c, (n + 1) * c)
            kv2 = lax.dot_general(k_ref[rows, cols], v_ref[rows, cols], contract_rows,
                                  preferred_element_type=F32)
            kv[hp, n] = (kv2[:d, :d], kv2[d:, d:])
    seen = {}
    for hp in range(pairs):
        for hh in range(2):
            h = 2 * hp + hh
            st = state_ref[h]
            for n in range(steps):
                seen[h, n] = (math.exp(log_g[h]) * st).astype(BF16)
                st = math.exp(c * log_g[h]) * st + math.exp((c - 1) * log_g[h]) * kv[hp, n][hh]
            state_ref[h] = st
    tri2 = jnp.concatenate([tri_ref[...], tri_ref[...]], axis=1)
    for hp in range(pairs):
        cols = slice(2 * hp * d, 2 * (hp + 1) * d)
        for n in range(steps):
            rows = slice(n * c, (n + 1) * c)
            q2, k2, v2 = q_ref[rows, cols], k_ref[rows, cols], v_ref[rows, cols]
            kbd = _block_diag(k2[:, :d], k2[:, d:])
            s2 = lax.dot_general(q2, kbd, (((1,), (1,)), ((), ())), preferred_element_type=F32) * tri2
            inner = jnp.dot(s2.astype(BF16), _block_diag(v2[:, :d], v2[:, d:]), preferred_element_type=F32)
            cross = jnp.dot(q2, _block_diag(seen[2 * hp, n], seen[2 * hp + 1, n]), preferred_element_type=F32)
            o2 = inner + cross
            for hh in range(2):
                hc = slice((2 * hp + hh) * d, (2 * hp + hh + 1) * d)
                o = o2[:, hh * d:(hh + 1) * d]
                mu = jnp.mean(o, axis=-1, keepdims=True)
                oc = o - mu
                var = jnp.mean(oc * oc, axis=-1, keepdims=True)
                gt = g_ref[rows, hc].astype(F32)
                o_ref[rows, hc] = (oc * lax.rsqrt(var + EPS) * gn_ref[:, hc] * (gt * _sigmoid(gt))).astype(BF16)


def _retention(rq, rk, rv, rg, gn):
    bsz, t, w = rq.shape
    tok = pl.BlockSpec((None, RET_TC, w), lambda b, i: (b, i, 0))
    return pl.pallas_call(
        _ret_kernel,
        out_shape=jax.ShapeDtypeStruct((bsz, t, w), BF16),
        grid=(bsz, t // RET_TC),
        in_specs=[tok, tok, tok, tok, pl.BlockSpec((1, w), lambda b, i: (0, 0))],
        out_specs=tok,
        scratch_shapes=[
            pltpu.VMEM((RET_HEADS, RET_HEAD_DIM, RET_HEAD_DIM), F32),
            pltpu.VMEM((RET_STEP, RET_STEP), F32),
        ],
        compiler_params=pltpu.CompilerParams(
            dimension_semantics=("arbitrary", "arbitrary"), vmem_limit_bytes=VMEM_LIMIT),
        name="retention",
    )(rq, rk, rv, rg, gn)


def _attn_unit(q2, k2, v2, bias, lo):
    blk = ATTN_BLOCK
    zero = jnp.zeros_like(q2)
    qs = jnp.concatenate([jnp.where(lo, q2, zero), jnp.where(lo, zero, q2)], axis=0)
    sc = lax.dot_general(qs, k2, (((1,), (1,)), ((), ())), preferred_element_type=F32)
    sc = sc + jnp.concatenate([bias, bias], axis=0)
    m = jnp.max(sc, axis=-1, keepdims=True)
    p = jnp.exp2(sc - m).astype(BF16)
    va = jnp.concatenate([v2, jnp.ones_like(v2)], axis=1)
    oa = jnp.dot(p, va, preferred_element_type=F32)
    o = jnp.where(lo, oa[:blk, :LANES], oa[blk:, :LANES])
    l = jnp.where(lo, oa[:blk, LANES:], oa[blk:, LANES:])
    mm = jnp.where(lo, jnp.broadcast_to(m[:blk], (blk, LANES)), jnp.broadcast_to(m[blk:], (blk, LANES)))
    return o / l, mm + jnp.log2(l)


def _attn_kernel(q1_ref, q4_ref, q16_ref,
                 k1p_ref, k1c_ref, k4p_ref, k4c_ref, k16p_ref, k16c_ref,
                 v1p_ref, v1c_ref, v4p_ref, v4c_ref, v16p_ref, v16c_ref,
                 g_ref, o_ref, bias_ref, ob_ref, lb_ref):
    tt, blk = ATTN_TT, ATTN_BLOCK
    tile = pl.program_id(2)
    lane = lax.broadcasted_iota(jnp.int32, (blk, LANES), 1)
    lo = lane < ATTN_HEAD_DIM
    qi = lax.broadcasted_iota(jnp.int32, (blk, 2 * blk), 0)
    kj = lax.broadcasted_iota(jnp.int32, (blk, 2 * blk), 1)
    neg = jnp.float32(-jnp.inf)
    band = jnp.where((kj >= qi) & (kj <= qi + blk), 0.0, neg)
    bias_ref[0] = band
    bias_ref[1] = jnp.where(jnp.logical_or(kj >= blk, tile > 0), band, neg)

    q_refs = (q1_ref, q4_ref, q16_ref)
    k_refs = ((k1p_ref, k1c_ref), (k4p_ref, k4c_ref), (k16p_ref, k16c_ref))
    v_refs = ((v1p_ref, v1c_ref), (v4p_ref, v4c_ref), (v16p_ref, v16c_ref))

    def rows_of(ref, r, s, lo_row, n):
        return ref[lo_row:lo_row + n, :] if r == 1 else ref[s, lo_row:lo_row + n, :]

    def keys(refs, r, s, j):
        prev_ref, cur_ref = refs
        if j == 0:
            return jnp.concatenate([rows_of(prev_ref, r, s, 0, blk), rows_of(cur_ref, r, s, 0, blk)], axis=0)
        return rows_of(cur_ref, r, s, (j - 1) * blk, 2 * blk)

    for bi, (window, r) in enumerate(DILATED_PATTERNS):
        assert window // r == blk
        for s in range(r):
            for j in range(tt // (blk * r)):
                q2 = rows_of(q_refs[bi], r, s, j * blk, blk)
                k2 = keys(k_refs[bi], r, s, j)
                v2 = keys(v_refs[bi], r, s, j)
                o, lse = _attn_unit(q2, k2, v2, bias_ref[1 if j == 0 else 0], lo)
                start = s + r * blk * j
                rows = pl.ds(start, blk, stride=r) if r > 1 else pl.ds(start, blk)
                ob_ref[bi, rows, :] = o
                lb_ref[bi, rows, :] = lse

    rc = 256
    lane_c = lax.broadcasted_iota(jnp.int32, (rc, LANES), 1)
    lo_c = lane_c < ATTN_HEAD_DIM
    gain = g_ref[...]

    for ci in range(tt // rc):
        rows = slice(ci * rc, (ci + 1) * rc)
        l0, l1, l2 = lb_ref[0, rows, :], lb_ref[1, rows, :], lb_ref[2, rows, :]
        mx = jnp.maximum(jnp.maximum(l0, l1), l2)
        e0, e1, e2 = jnp.exp2(l0 - mx), jnp.exp2(l1 - mx), jnp.exp2(l2 - mx)
        att = (e0 * ob_ref[0, rows, :] + e1 * ob_ref[1, rows, :] + e2 * ob_ref[2, rows, :]) / (e0 + e1 + e2)
        sq = att * att
        ms_lo = jnp.sum(jnp.where(lo_c, sq, 0.0), axis=-1, keepdims=True)
        ms_hi = jnp.sum(jnp.where(lo_c, 0.0, sq), axis=-1, keepdims=True)
        ms = jnp.where(lo_c, ms_lo, ms_hi) * (1.0 / ATTN_HEAD_DIM)
        o_ref[rows, :] = (att * lax.rsqrt(ms + EPS) * gain).astype(BF16)


def _dilated_attention(aq, ak, av, gain):
    bsz, t, w = aq[0].shape
    tt, blk = ATTN_TT, ATTN_BLOCK
    ngrp = w // LANES

    def cur_prev(r):
        rows = tt // r
        if r == 1:
            cur = pl.BlockSpec((None, rows, LANES), lambda b, hp, i: (b, i, hp))
            prev = pl.BlockSpec((None, blk, LANES),
                                lambda b, hp, i: (b, jnp.maximum(i * (rows // blk) - 1, 0), hp))
        else:
            cur = pl.BlockSpec((None, r, rows, LANES), lambda b, hp, i: (b, 0, i, hp))
            prev = pl.BlockSpec((None, r, blk, LANES),
                                lambda b, hp, i: (b, 0, jnp.maximum(i * (rows // blk) - 1, 0), hp))
        return cur, prev

    specs = [cur_prev(r) for _, r in DILATED_PATTERNS]
    q_specs = [c for c, _ in specs]
    kv_specs = [sp for c, p in specs for sp in (p, c)]
    kv_args = lambda a: [x for arr in a for x in (arr, arr)]
    return pl.pallas_call(
        _attn_kernel,
        out_shape=jax.ShapeDtypeStruct((bsz, t, w), BF16),
        grid=(bsz, ngrp, t // tt),
        in_specs=q_specs + kv_specs + kv_specs + [pl.BlockSpec((1, LANES), lambda b, hp, i: (0, hp))],
        out_specs=pl.BlockSpec((None, tt, LANES), lambda b, hp, i: (b, i, hp)),
        scratch_shapes=[
            pltpu.VMEM((2, blk, 2 * blk), F32),
            pltpu.VMEM((3, tt, LANES), F32), pltpu.VMEM((3, tt, LANES), F32),
        ],
        compiler_params=pltpu.CompilerParams(
            dimension_semantics=("arbitrary", "arbitrary", "arbitrary"), vmem_limit_bytes=VMEM_LIMIT),
        name="dilated_attention",
    )(*aq, *kv_args(ak), *kv_args(av), gain)


def kernel(x, c, w_ada, b_ada, norm1_g, ffn1_w_in, ffn1_w_out, norm_mix_g, w_in_mix, ret_gn_g, attn_norm_g,
           w_out_mix, norm2_g, ffn2_w_in, ffn2_w_out, norm_f_g):
    depth = w_ada.shape[0]
    assert depth >= 1, "the final RMSNorm is fused into the last layer's second FFN"
    bsz, t, d = x.shape
    gf = norm_f_g.reshape(1, d)
    for l in range(depth):
        last = l == depth - 1
        mod = _adaln_mod(c, w_ada[l], b_ada[l]).reshape(bsz, N_MOD, d)
        x = _ffn(x, mod, norm1_g[l].reshape(1, d), ffn1_w_in[l].astype(BF16), ffn1_w_out[l].astype(BF16), gf,
                 sub=0, final_norm=False)
        proj = _mixer_in_proj(x, mod, norm_mix_g[l].reshape(1, d), w_in_mix[l].astype(BF16))
        rq, rk, rv, rg = proj[:4]
        ret = _retention(rq, rk, rv, rg, ret_gn_g[l].reshape(1, RET_WIDTH))
        att = _dilated_attention(proj[4:7], proj[7:10], proj[10:13], attn_norm_g[l].reshape(1, ATTN_WIDTH))
        x = _ffn(x, mod, norm2_g[l].reshape(1, d), ffn2_w_in[l].astype(BF16), ffn2_w_out[l].astype(BF16), gf,
                 sub=2, final_norm=last, mixer=(ret, att, w_out_mix[l].astype(BF16)))
    return x
```

```python
import functools
import math

import jax
import jax.numpy as jnp
from jax import lax
from jax.experimental import pallas as pl
from jax.experimental.pallas import tpu as pltpu

D_MODEL = 1024
RET_WIDTH = 512
RET_HEAD_DIM = 128
RET_HEADS = 4
RET_CHUNK = 128
ATTN_WIDTH = 512
ATTN_HEAD_DIM = 64
DILATED_PATTERNS = ((128, 1), (512, 4), (2048, 16))
ATTN_BLOCK = 128
ROPE_THETA = 10000.0
D_FF = 2816
N_MOD = 9
IN_COLS = 4 * RET_WIDTH + 3 * ATTN_WIDTH
EPS = 1e-6
LOG2_E = math.log2(math.e)

LANES = 128
VMEM_LIMIT = 56 * 1024 * 1024

FFN_TM = 512
FFN_FC = 256
PROJ_TM = 512
RET_TC = 1024
RET_STEP = 256
ATTN_TT = ATTN_BLOCK * DILATED_PATTERNS[-1][1]
BF16 = jnp.bfloat16
F32 = jnp.float32


def _sigmoid(x):
    return 1.0 / (1.0 + jnp.exp(-x))


def _norm_mod(x, g, shift, scale):
    ms = jnp.mean(x * x, axis=-1, keepdims=True)
    return (x * lax.rsqrt(ms + EPS) * g) * (1.0 + scale) + shift


def _mod_kernel(c_ref, w_ref, b_ref, o_ref):
    c = c_ref[...]
    ca = c * _sigmoid(c)
    o_ref[...] = jnp.dot(ca, w_ref[...], preferred_element_type=F32) + b_ref[...]


def _adaln_mod(c, w, b):
    bsz, d = c.shape
    n = w.shape[1]
    tn = 1024
    return pl.pallas_call(
        _mod_kernel,
        out_shape=jax.ShapeDtypeStruct((bsz, n), F32),
        grid=(n // tn,),
        in_specs=[
            pl.BlockSpec((bsz, d), lambda j: (0, 0)),
            pl.BlockSpec((d, tn), lambda j: (0, j)),
            pl.BlockSpec((1, tn), lambda j: (0, j)),
        ],
        out_specs=pl.BlockSpec((bsz, tn), lambda j: (0, j)),
        compiler_params=pltpu.CompilerParams(dimension_semantics=("arbitrary",)),
        name="adaln_mod",
    )(c, w, b.reshape(1, n))


def _ffn_kernel(*refs, sub, mixer_out, final_norm):
    if mixer_out:
        x_ref, ret_ref, att_ref, wmix_ref, *refs = refs
    else:
        x_ref, *refs = refs
    mod_ref, g_ref, win_ref, wout_ref, gf_ref, o_ref, h_ref, hn_ref, xs_ref, xsn_ref, act_ref, act0_ref = refs
    shift = mod_ref[3 * sub:3 * sub + 1, :]
    scale = mod_ref[3 * sub + 1:3 * sub + 2, :]
    gate = mod_ref[3 * sub + 2:3 * sub + 3, :]

    def hidden_chunk(h, j):
        a = jnp.dot(h, win_ref[:, j * FFN_FC:(j + 1) * FFN_FC], preferred_element_type=F32)
        b = jnp.dot(h, win_ref[:, D_FF + j * FFN_FC:D_FF + (j + 1) * FFN_FC], preferred_element_type=F32)
        return (a * _sigmoid(a) * b).astype(BF16)

    def start_tile(h_out, x_out, act0_out):
        x = x_ref[...]
        if mixer_out:
            y = jnp.dot(ret_ref[...], wmix_ref[0:RET_WIDTH, :], preferred_element_type=F32)
            y = y + jnp.dot(att_ref[...], wmix_ref[RET_WIDTH:, :], preferred_element_type=F32)
            x = x + mod_ref[5:6, :] * y
        x_out[...] = x
        h_out[...] = _norm_mod(x, g_ref[...], shift, scale).astype(BF16)
        act0_out[...] = hidden_chunk(h_out[...], 0)

    @pl.when(pl.program_id(1) == 0)
    def _():
        start_tile(h_ref, xs_ref, act_ref.at[:, 0:FFN_FC])

    @pl.when(pl.program_id(1) > 0)
    def _():
        for j in range(1, D_FF // FFN_FC):
            act_ref[:, j * FFN_FC:(j + 1) * FFN_FC] = hidden_chunk(h_ref[...], j)
        y = jnp.dot(act_ref[...], wout_ref[...], preferred_element_type=F32)
        xn = xs_ref[...] + 0.5 * gate * y
        if final_norm:
            ms = jnp.mean(xn * xn, axis=-1, keepdims=True)
            xn = xn * lax.rsqrt(ms + EPS) * gf_ref[...]
        o_ref[...] = xn
        start_tile(hn_ref, xsn_ref, act0_ref)
        h_ref[...] = hn_ref[...]
        xs_ref[...] = xsn_ref[...]
        act_ref[:, 0:FFN_FC] = act0_ref[...]


def _ffn(x, mod, g, w_in, w_out, g_final, *, sub, final_norm, mixer=None):
    bsz, t, d = x.shape
    tm = FFN_TM
    kern = functools.partial(_ffn_kernel, sub=sub, mixer_out=mixer is not None, final_norm=final_norm)
    n = t // tm
    tok = lambda b, i: (b, jnp.minimum(i, n - 1), 0)
    mixer_specs, mixer_args = [], []
    if mixer is not None:
        ret, att, w_mix = mixer
        mixer_specs = [pl.BlockSpec((None, tm, ret.shape[-1]), tok), pl.BlockSpec((None, tm, att.shape[-1]), tok),
                       pl.BlockSpec(w_mix.shape, lambda b, i: (0, 0), pipeline_mode=pl.Buffered(1))]
        mixer_args = [ret, att, w_mix]
    return pl.pallas_call(
        kern,
        out_shape=jax.ShapeDtypeStruct((bsz, t, d), F32),
        grid=(bsz, n + 1),
        in_specs=[pl.BlockSpec((None, tm, d), tok)] + mixer_specs + [
            pl.BlockSpec((None, N_MOD, d), lambda b, i: (b, 0, 0)),
            pl.BlockSpec((1, d), lambda b, i: (0, 0)),
            pl.BlockSpec((d, 2 * D_FF), lambda b, i: (0, 0), pipeline_mode=pl.Buffered(1)),
            pl.BlockSpec((D_FF, d), lambda b, i: (0, 0), pipeline_mode=pl.Buffered(1)),
            pl.BlockSpec((1, d), lambda b, i: (0, 0)),
        ],
        out_specs=pl.BlockSpec((None, tm, d), lambda b, i: (b, jnp.maximum(i - 1, 0), 0)),
        scratch_shapes=[pltpu.VMEM((tm, d), BF16), pltpu.VMEM((tm, d), BF16),
                        pltpu.VMEM((tm, d), F32), pltpu.VMEM((tm, d), F32),
                        pltpu.VMEM((tm, D_FF), BF16), pltpu.VMEM((tm, FFN_FC), BF16)],
        compiler_params=pltpu.CompilerParams(
            dimension_semantics=("arbitrary", "arbitrary"), vmem_limit_bytes=VMEM_LIMIT),
        name=f"ffn{sub // 2 + 1}",
    )(x, *mixer_args, mod, g, w_in, w_out, g_final)


def _store_residue_layouts(val, nat_ref, p4_ref, p16_ref, t1_ref, t4_ref):
    tm = val.shape[0]
    nat_ref[...] = val.astype(BF16)
    for g in range(val.shape[1] // LANES):
        cols = slice(g * LANES, (g + 1) * LANES)
        t1_ref[g] = val[:, cols]
        for s4 in range(4):
            y = t1_ref[g, pl.ds(s4, tm // 4, stride=4), :]
            p4_ref[s4, :, cols] = y.astype(BF16)
            t4_ref[g, s4 * (tm // 4):(s4 + 1) * (tm // 4), :] = y
        for s16 in range(16):
            s4, u = s16 % 4, s16 // 4
            y = t4_ref[g, pl.ds(s4 * (tm // 4) + u, tm // 16, stride=4), :]
            p16_ref[s16, :, cols] = y.astype(BF16)


def _proj_kernel(x_ref, mod_ref, g_ref, w_ref, tr_ref, br_ref, ta_ref, ba_ref,
                 rq_ref, rk_ref, rv_ref, rg_ref,
                 aq1_ref, aq4_ref, aq16_ref, ak1_ref, ak4_ref, ak16_ref, av1_ref, av4_ref, av16_ref,
                 h_ref, t1_ref, t4_ref):
    x = x_ref[...]
    h_ref[...] = _norm_mod(x, g_ref[...], mod_ref[3:4, :], mod_ref[4:5, :]).astype(BF16)

    def proj(c):
        return jnp.dot(h_ref[...], w_ref[:, c * 512:(c + 1) * 512], preferred_element_type=F32)

    def rope_ret(p, t_cos, t_sin):
        cb, sb, cbs, sbs = br_ref[0:1, :], br_ref[1:2, :], br_ref[2:3, :], br_ref[3:4, :]
        cos = t_cos * cb - t_sin * sb
        sin = t_cos * sbs + t_sin * cbs
        outs = []
        for hh in range(RET_HEADS):
            cols = slice(hh * LANES, (hh + 1) * LANES)
            xh = p[:, cols]
            outs.append(xh * cos[:, cols] + pltpu.roll(xh, RET_HEAD_DIM // 2, 1) * sin[:, cols])
        return jnp.concatenate(outs, axis=1)

    def rope_attn(p):
        t_cos, t_sin = ta_ref[0], ta_ref[1]
        cos = t_cos * ba_ref[0:1, :] - t_sin * ba_ref[1:2, :]
        sin_lo = t_cos * ba_ref[3:4, :] + t_sin * ba_ref[2:3, :]
        sin_hi = t_cos * ba_ref[5:6, :] + t_sin * ba_ref[4:5, :]
        outs = []
        for hh in range(ATTN_WIDTH // LANES):
            xh = p[:, hh * LANES:(hh + 1) * LANES]
            outs.append(xh * cos + pltpu.roll(xh, 96, 1) * sin_lo + pltpu.roll(xh, 32, 1) * sin_hi)
        return jnp.concatenate(outs, axis=1)

    _store_residue_layouts(rope_attn(proj(4)) * (LOG2_E * ATTN_HEAD_DIM ** -0.5),
                           aq1_ref, aq4_ref, aq16_ref, t1_ref.at[0], t4_ref.at[0])
    _store_residue_layouts(rope_attn(proj(5)), ak1_ref, ak4_ref, ak16_ref, t1_ref.at[1], t4_ref.at[1])
    _store_residue_layouts(proj(6), av1_ref, av4_ref, av16_ref, t1_ref.at[2], t4_ref.at[2])
    rq_ref[...] = rope_ret(proj(0), tr_ref[0], tr_ref[1]).astype(BF16)
    rk_ref[...] = rope_ret(proj(1), tr_ref[2], tr_ref[3]).astype(BF16)
    rv_ref[...] = proj(2).astype(BF16)
    rg_ref[...] = proj(3).astype(BF16)


def _rope_tables(t, tm):
    assert tm % RET_STEP == 0
    p = jnp.arange(tm, dtype=F32)[:, None]
    base = (jnp.arange(t // tm, dtype=F32) * tm)[:, None]
    ones = jnp.ones((RET_HEAD_DIM // 2,), F32)
    inv_r = ROPE_THETA ** (-jnp.arange(0, RET_HEAD_DIM, 2, dtype=F32) / RET_HEAD_DIM)
    f_r = jnp.concatenate([inv_r, inv_r])
    cos_p, sin_p = jnp.cos(p * f_r), jnp.sin(p * f_r)
    loc = p % RET_STEP
    log_g = [math.log1p(-(2.0 ** (-5.0 - hh))) for hh in range(RET_HEADS)]
    dq = [jnp.exp(loc * lg) for lg in log_g]
    dk = [jnp.exp(-loc * lg) * RET_HEAD_DIM ** -0.5 for lg in log_g]
    per_head = lambda scales, tab: jnp.concatenate([s * tab for s in scales], axis=1)
    tab_r = jnp.stack([per_head(dq, cos_p), per_head(dq, sin_p), per_head(dk, cos_p), per_head(dk, sin_p)])
    sign = jnp.concatenate([-ones, ones])
    cos_b, sin_b = jnp.cos(base * f_r), jnp.sin(base * f_r)
    heads = lambda a: jnp.concatenate([a] * RET_HEADS, axis=1)
    base_r = jnp.stack([heads(cos_b), heads(sin_b), heads(sign * cos_b), heads(sign * sin_b)], axis=1)
    inv_a = ROPE_THETA ** (-jnp.arange(0, ATTN_HEAD_DIM, 2, dtype=F32) / ATTN_HEAD_DIM)
    f_a = jnp.concatenate([inv_a] * 4)
    tab_a = jnp.stack([jnp.cos(p * f_a), jnp.sin(p * f_a)])
    half = jnp.ones((ATTN_HEAD_DIM // 2,), F32)
    pat_lo = jnp.concatenate([-half, 0 * half, -half, 0 * half])
    pat_hi = jnp.concatenate([0 * half, half, 0 * half, half])
    cos_ba, sin_ba = jnp.cos(base * f_a), jnp.sin(base * f_a)
    zero = jnp.zeros_like(cos_ba)
    base_a = jnp.stack([cos_ba, sin_ba, pat_lo * cos_ba, pat_lo * sin_ba, pat_hi * cos_ba, pat_hi * sin_ba,
                        zero, zero], axis=1)
    return tab_r, base_r, tab_a, base_a


def _mixer_in_proj(x, mod, g, w):
    bsz, t, d = x.shape
    tm = PROJ_TM
    tab_r, base_r, tab_a, base_a = _rope_tables(t, tm)
    tok = lambda b, i: (b, i, 0)
    const3 = lambda b, i: (0, 0, 0)
    aw = ATTN_WIDTH
    nat = jax.ShapeDtypeStruct((bsz, t, aw), BF16)
    p4 = jax.ShapeDtypeStruct((bsz, 4, t // 4, aw), BF16)
    p16 = jax.ShapeDtypeStruct((bsz, 16, t // 16, aw), BF16)
    nat_spec = pl.BlockSpec((None, tm, aw), tok)
    p4_spec = pl.BlockSpec((None, 4, tm // 4, aw), lambda b, i: (b, 0, i, 0))
    p16_spec = pl.BlockSpec((None, 16, tm // 16, aw), lambda b, i: (b, 0, i, 0))
    return pl.pallas_call(
        _proj_kernel,
        out_shape=[nat] * 4 + [nat, p4, p16] * 3,
        grid=(bsz, t // tm),
        in_specs=[
            pl.BlockSpec((None, tm, d), tok),
            pl.BlockSpec((None, N_MOD, d), lambda b, i: (b, 0, 0)),
            pl.BlockSpec((1, d), lambda b, i: (0, 0)),
            pl.BlockSpec((d, IN_COLS), lambda b, i: (0, 0), pipeline_mode=pl.Buffered(1)),
            pl.BlockSpec(tab_r.shape, const3, pipeline_mode=pl.Buffered(1)),
            pl.BlockSpec((None,) + base_r.shape[1:], lambda b, i: (i, 0, 0)),
            pl.BlockSpec(tab_a.shape, const3, pipeline_mode=pl.Buffered(1)),
            pl.BlockSpec((None,) + base_a.shape[1:], lambda b, i: (i, 0, 0)),
        ],
        out_specs=[nat_spec] * 4 + [nat_spec, p4_spec, p16_spec] * 3,
        scratch_shapes=[pltpu.VMEM((tm, d), BF16),
                        pltpu.VMEM((3, aw // LANES, tm, LANES), F32), pltpu.VMEM((3, aw // LANES, tm, LANES), F32)],
        compiler_params=pltpu.CompilerParams(
            dimension_semantics=("arbitrary", "arbitrary"), vmem_limit_bytes=VMEM_LIMIT),
        name="mixer_in_proj",
    )(x, mod, g, w, tab_r, base_r, tab_a, base_a)


def _block_diag(a, b):
    za, zb = jnp.zeros_like(a), jnp.zeros_like(b)
    return jnp.concatenate([jnp.concatenate([a, zb], axis=1), jnp.concatenate([za, b], axis=1)], axis=0)


def _ret_kernel(q_ref, k_ref, v_ref, g_ref, gn_ref, o_ref, state_ref, tri_ref):
    c, d = RET_STEP, RET_HEAD_DIM
    log_g = [math.log1p(-(2.0 ** (-5.0 - hh))) for hh in range(RET_HEADS)]

    @pl.when(jnp.logical_and(pl.program_id(0) == 0, pl.program_id(1) == 0))
    def _():
        ii = lax.broadcasted_iota(jnp.int32, (c, c), 0)
        jj = lax.broadcasted_iota(jnp.int32, (c, c), 1)
        tri_ref[...] = jnp.where(ii >= jj, 1.0, 0.0)

    @pl.when(pl.program_id(1) == 0)
    def _():
        state_ref[...] = jnp.zeros_like(state_ref)

    pairs = RET_HEADS // 2
    steps = RET_TC // c
    contract_rows = (((0,), (0,)), ((), ()))
    kv = {}
    for hp in range(pairs):
        cols = slice(2 * hp * d, 2 * (hp + 1) * d)
        for n in range(steps):
            rows = slice(n * c, (n + 1) * c)
            kv2 = lax.dot_general(k_ref[rows, cols], v_ref[rows, cols], contract_rows,
                                  preferred_element_type=F32)
            kv[hp, n] = (kv2[:d, :d], kv2[d:, d:])
    seen = {}
    for hp in range(pairs):
        for hh in range(2):
            h = 2 * hp + hh
            st = state_ref[h]
            for n in range(steps):
                seen[h, n] = (math.exp(log_g[h]) * st).astype(BF16)
                st = math.exp(c * log_g[h]) * st + math.exp((c - 1) * log_g[h]) * kv[hp, n][hh]
            state_ref[h] = st
    tri2 = jnp.concatenate([tri_ref[...], tri_ref[...]], axis=1)
    for hp in range(pairs):
        cols = slice(2 * hp * d, 2 * (hp + 1) * d)
        for n in range(steps):
            rows = slice(n * c, (n + 1) * c)
            q2, k2, v2 = q_ref[rows, cols], k_ref[rows, cols], v_ref[rows, cols]
            kbd = _block_diag(k2[:, :d], k2[:, d:])
            s2 = lax.dot_general(q2, kbd, (((1,), (1,)), ((), ())), preferred_element_type=F32) * tri2
            inner = jnp.dot(s2.astype(BF16), _block_diag(v2[:, :d], v2[:, d:]), preferred_element_type=F32)
            cross = jnp.dot(q2, _block_diag(seen[2 * hp, n], seen[2 * hp + 1, n]), preferred_element_type=F32)
            o2 = inner + cross
            for hh in range(2):
                hc = slice((2 * hp + hh) * d, (2 * hp + hh + 1) * d)
                o = o2[:, hh * d:(hh + 1) * d]
                mu = jnp.mean(o, axis=-1, keepdims=True)
                oc = o - mu
                var = jnp.mean(oc * oc, axis=-1, keepdims=True)
                gt = g_ref[rows, hc].astype(F32)
                o_ref[rows, hc] = (oc * lax.rsqrt(var + EPS) * gn_ref[:, hc] * (gt * _sigmoid(gt))).astype(BF16)


def _retention(rq, rk, rv, rg, gn):
    bsz, t, w = rq.shape
    tok = pl.BlockSpec((None, RET_TC, w), lambda b, i: (b, i, 0))
    return pl.pallas_call(
        _ret_kernel,
        out_shape=jax.ShapeDtypeStruct((bsz, t, w), BF16),
        grid=(bsz, t // RET_TC),
        in_specs=[tok, tok, tok, tok, pl.BlockSpec((1, w), lambda b, i: (0, 0))],
        out_specs=tok,
        scratch_shapes=[
            pltpu.VMEM((RET_HEADS, RET_HEAD_DIM, RET_HEAD_DIM), F32),
            pltpu.VMEM((RET_STEP, RET_STEP), F32),
        ],
        compiler_params=pltpu.CompilerParams(
            dimension_semantics=("arbitrary", "arbitrary"), vmem_limit_bytes=VMEM_LIMIT),
        name="retention",
    )(rq, rk, rv, rg, gn)


def _attn_unit(q2, k2, v2, bias, lo):
    blk = ATTN_BLOCK
    zero = jnp.zeros_like(q2)
    qs = jnp.concatenate([jnp.where(lo, q2, zero), jnp.where(lo, zero, q2)], axis=0)
    sc = lax.dot_general(qs, k2, (((1,), (1,)), ((), ())), preferred_element_type=F32)
    sc = sc + jnp.concatenate([bias, bias], axis=0)
    m = jnp.max(sc, axis=-1, keepdims=True)
    p = jnp.exp2(sc - m).astype(BF16)
    va = jnp.concatenate([v2, jnp.ones_like(v2)], axis=1)
    oa = jnp.dot(p, va, preferred_element_type=F32)
    o = jnp.where(lo, oa[:blk, :LANES], oa[blk:, :LANES])
    l = jnp.where(lo, oa[:blk, LANES:], oa[blk:, LANES:])
    mm = jnp.where(lo, jnp.broadcast_to(m[:blk], (blk, LANES)), jnp.broadcast_to(m[blk:], (blk, LANES)))
    return o / l, mm + jnp.log2(l)


def _attn_kernel(q1_ref, q4_ref, q16_ref,
                 k1p_ref, k1c_ref, k4p_ref, k4c_ref, k16p_ref, k16c_ref,
                 v1p_ref, v1c_ref, v4p_ref, v4c_ref, v16p_ref, v16c_ref,
                 g_ref, o_ref, bias_ref, ob_ref, lb_ref):
    tt, blk = ATTN_TT, ATTN_BLOCK
    tile = pl.program_id(2)
    lane = lax.broadcasted_iota(jnp.int32, (blk, LANES), 1)
    lo = lane < ATTN_HEAD_DIM
    qi = lax.broadcasted_iota(jnp.int32, (blk, 2 * blk), 0)
    kj = lax.broadcasted_iota(jnp.int32, (blk, 2 * blk), 1)
    neg = jnp.float32(-jnp.inf)
    band = jnp.where((kj >= qi) & (kj <= qi + blk), 0.0, neg)
    bias_ref[0] = band
    bias_ref[1] = jnp.where(jnp.logical_or(kj >= blk, tile > 0), band, neg)

    q_refs = (q1_ref, q4_ref, q16_ref)
    k_refs = ((k1p_ref, k1c_ref), (k4p_ref, k4c_ref), (k16p_ref, k16c_ref))
    v_refs = ((v1p_ref, v1c_ref), (v4p_ref, v4c_ref), (v16p_ref, v16c_ref))

    def rows_of(ref, r, s, lo_row, n):
        return ref[lo_row:lo_row + n, :] if r == 1 else ref[s, lo_row:lo_row + n, :]

    def keys(refs, r, s, j):
        prev_ref, cur_ref = refs
        if j == 0:
            return jnp.concatenate([rows_of(prev_ref, r, s, 0, blk), rows_of(cur_ref, r, s, 0, blk)], axis=0)
        return rows_of(cur_ref, r, s, (j - 1) * blk, 2 * blk)

    for bi, (window, r) in enumerate(DILATED_PATTERNS):
        assert window // r == blk
        for s in range(r):
            for j in range(tt // (blk * r)):
                q2 = rows_of(q_refs[bi], r, s, j * blk, blk)
                k2 = keys(k_refs[bi], r, s, j)
                v2 = keys(v_refs[bi], r, s, j)
                o, lse = _attn_unit(q2, k2, v2, bias_ref[1 if j == 0 else 0], lo)
                start = s + r * blk * j
                rows = pl.ds(start, blk, stride=r) if r > 1 else pl.ds(start, blk)
                ob_ref[bi, rows, :] = o
                lb_ref[bi, rows, :] = lse

    rc = 256
    lane_c = lax.broadcasted_iota(jnp.int32, (rc, LANES), 1)
    lo_c = lane_c < ATTN_HEAD_DIM
    gain = g_ref[...]

    for ci in range(tt // rc):
        rows = slice(ci * rc, (ci + 1) * rc)
        l0, l1, l2 = lb_ref[0, rows, :], lb_ref[1, rows, :], lb_ref[2, rows, :]
        mx = jnp.maximum(jnp.maximum(l0, l1), l2)
        e0, e1, e2 = jnp.exp2(l0 - mx), jnp.exp2(l1 - mx), jnp.exp2(l2 - mx)
        att = (e0 * ob_ref[0, rows, :] + e1 * ob_ref[1, rows, :] + e2 * ob_ref[2, rows, :]) / (e0 + e1 + e2)
        sq = att * att
        ms_lo = jnp.sum(jnp.where(lo_c, sq, 0.0), axis=-1, keepdims=True)
        ms_hi = jnp.sum(jnp.where(lo_c, 0.0, sq), axis=-1, keepdims=True)
        ms = jnp.where(lo_c, ms_lo, ms_hi) * (1.0 / ATTN_HEAD_DIM)
        o_ref[rows, :] = (att * lax.rsqrt(ms + EPS) * gain).astype(BF16)


def _dilated_attention(aq, ak, av, gain):
    bsz, t, w = aq[0].shape
    tt, blk = ATTN_TT, ATTN_BLOCK
    ngrp = w // LANES

    def cur_prev(r):
        rows = tt // r
        if r == 1:
            cur = pl.BlockSpec((None, rows, LANES), lambda b, hp, i: (b, i, hp))
            prev = pl.BlockSpec((None, blk, LANES),
                                lambda b, hp, i: (b, jnp.maximum(i * (rows // blk) - 1, 0), hp))
        else:
            cur = pl.BlockSpec((None, r, rows, LANES), lambda b, hp, i: (b, 0, i, hp))
            prev = pl.BlockSpec((None, r, blk, LANES),
                                lambda b, hp, i: (b, 0, jnp.maximum(i * (rows // blk) - 1, 0), hp))
        return cur, prev

    specs = [cur_prev(r) for _, r in DILATED_PATTERNS]
    q_specs = [c for c, _ in specs]
    kv_specs = [sp for c, p in specs for sp in (p, c)]
    kv_args = lambda a: [x for arr in a for x in (arr, arr)]
    return pl.pallas_call(
        _attn_kernel,
        out_shape=jax.ShapeDtypeStruct((bsz, t, w), BF16),
        grid=(bsz, ngrp, t // tt),
        in_specs=q_specs + kv_specs + kv_specs + [pl.BlockSpec((1, LANES), lambda b, hp, i: (0, hp))],
        out_specs=pl.BlockSpec((None, tt, LANES), lambda b, hp, i: (b, i, hp)),
        scratch_shapes=[
            pltpu.VMEM((2, blk, 2 * blk), F32),
            pltpu.VMEM((3, tt, LANES), F32), pltpu.VMEM((3, tt, LANES), F32),
        ],
        compiler_params=pltpu.CompilerParams(
            dimension_semantics=("arbitrary", "arbitrary", "arbitrary"), vmem_limit_bytes=VMEM_LIMIT),
        name="dilated_attention",
    )(*aq, *kv_args(ak), *kv_args(av), gain)


def kernel(x, c, w_ada, b_ada, norm1_g, ffn1_w_in, ffn1_w_out, norm_mix_g, w_in_mix, ret_gn_g, attn_norm_g,
           w_out_mix, norm2_g, ffn2_w_in, ffn2_w_out, norm_f_g):
    depth = w_ada.shape[0]
    assert depth >= 1, "the final RMSNorm is fused into the last layer's second FFN"
    bsz, t, d = x.shape
    gf = norm_f_g.reshape(1, d)
    for l in range(depth):
        last = l == depth - 1
        mod = _adaln_mod(c, w_ada[l], b_ada[l]).reshape(bsz, N_MOD, d)
        x = _ffn(x, mod, norm1_g[l].reshape(1, d), ffn1_w_in[l].astype(BF16), ffn1_w_out[l].astype(BF16), gf,
                 sub=0, final_norm=False)
        proj = _mixer_in_proj(x, mod, norm_mix_g[l].reshape(1, d), w_in_mix[l].astype(BF16))
        rq, rk, rv, rg = proj[:4]
        ret = _retention(rq, rk, rv, rg, ret_gn_g[l].reshape(1, RET_WIDTH))
        att = _dilated_attention(proj[4:7], proj[7:10], proj[10:13], attn_norm_g[l].reshape(1, ATTN_WIDTH))
        x = _ffn(x, mod, norm2_g[l].reshape(1, d), ffn2_w_in[l].astype(BF16), ffn2_w_out[l].astype(BF16), gf,
                 sub=2, final_norm=last, mixer=(ret, att, w_out_mix[l].astype(BF16)))
    return x
```

```python
import functools
import math

import jax
import jax.numpy as jnp
from jax import lax
from jax.experimental import pallas as pl
from jax.experimental.pallas import tpu as pltpu

D_MODEL = 1024
RET_WIDTH = 512
RET_HEAD_DIM = 128
RET_HEADS = 4
RET_CHUNK = 128
ATTN_WIDTH = 512
ATTN_HEAD_DIM = 64
DILATED_PATTERNS = ((128, 1), (512, 4), (2048, 16))
ATTN_BLOCK = 128
ROPE_THETA = 10000.0
D_FF = 2816
N_MOD = 9
IN_COLS = 4 * RET_WIDTH + 3 * ATTN_WIDTH
EPS = 1e-6
LOG2_E = math.log2(math.e)

LANES = 128
VMEM_LIMIT = 56 * 1024 * 1024

FFN_TM = 512
FFN_FC = 256
PROJ_TM = 512
RET_TC = 1024
RET_STEP = 256
ATTN_TT = ATTN_BLOCK * DILATED_PATTERNS[-1][1]
BF16 = jnp.bfloat16
F32 = jnp.float32


def _sigmoid(x):
    return 1.0 / (1.0 + jnp.exp(-x))


def _norm_mod(x, g, shift, scale):
    ms = jnp.mean(x * x, axis=-1, keepdims=True)
    return (x * lax.rsqrt(ms + EPS) * g) * (1.0 + scale) + shift


def _mod_kernel(c_ref, w_ref, b_ref, o_ref):
    c = c_ref[...]
    ca = c * _sigmoid(c)
    o_ref[...] = jnp.dot(ca, w_ref[...], preferred_element_type=F32) + b_ref[...]


def _adaln_mod(c, w, b):
    bsz, d = c.shape
    n = w.shape[1]
    tn = 1024
    return pl.pallas_call(
        _mod_kernel,
        out_shape=jax.ShapeDtypeStruct((bsz, n), F32),
        grid=(n // tn,),
        in_specs=[
            pl.BlockSpec((bsz, d), lambda j: (0, 0)),
            pl.BlockSpec((d, tn), lambda j: (0, j)),
            pl.BlockSpec((1, tn), lambda j: (0, j)),
        ],
        out_specs=pl.BlockSpec((bsz, tn), lambda j: (0, j)),
        compiler_params=pltpu.CompilerParams(dimension_semantics=("arbitrary",)),
        name="adaln_mod",
    )(c, w, b.reshape(1, n))


def _ffn_kernel(*refs, sub, mixer_out, final_norm):
    if mixer_out:
        x_ref, ret_ref, att_ref, wmix_ref, *refs = refs
    else:
        x_ref, *refs = refs
    mod_ref, g_ref, win_ref, wout_ref, gf_ref, o_ref, h_ref, act_ref = refs
    x = x_ref[...]
    if mixer_out:
        y = jnp.dot(ret_ref[...], wmix_ref[0:RET_WIDTH, :], preferred_element_type=F32)
        y = y + jnp.dot(att_ref[...], wmix_ref[RET_WIDTH:, :], preferred_element_type=F32)
        x = x + mod_ref[5:6, :] * y
    shift = mod_ref[3 * sub:3 * sub + 1, :]
    scale = mod_ref[3 * sub + 1:3 * sub + 2, :]
    gate = mod_ref[3 * sub + 2:3 * sub + 3, :]
    h_ref[...] = _norm_mod(x, g_ref[...], shift, scale).astype(BF16)
    for j in range(D_FF // FFN_FC):
        h = h_ref[...]
        a = jnp.dot(h, win_ref[:, j * FFN_FC:(j + 1) * FFN_FC], preferred_element_type=F32)
        b = jnp.dot(h, win_ref[:, D_FF + j * FFN_FC:D_FF + (j + 1) * FFN_FC], preferred_element_type=F32)
        act_ref[:, j * FFN_FC:(j + 1) * FFN_FC] = (a * _sigmoid(a) * b).astype(BF16)
    y = jnp.dot(act_ref[...], wout_ref[...], preferred_element_type=F32)
    xn = x + 0.5 * gate * y
    if final_norm:
        ms = jnp.mean(xn * xn, axis=-1, keepdims=True)
        xn = xn * lax.rsqrt(ms + EPS) * gf_ref[...]
    o_ref[...] = xn


def _ffn(x, mod, g, w_in, w_out, g_final, *, sub, final_norm, mixer=None):
    bsz, t, d = x.shape
    tm = FFN_TM
    kern = functools.partial(_ffn_kernel, sub=sub, mixer_out=mixer is not None, final_norm=final_norm)
    tok = lambda b, i: (b, i, 0)
    mixer_specs, mixer_args = [], []
    if mixer is not None:
        ret, att, w_mix = mixer
        mixer_specs = [pl.BlockSpec((None, tm, ret.shape[-1]), tok), pl.BlockSpec((None, tm, att.shape[-1]), tok),
                       pl.BlockSpec(w_mix.shape, lambda b, i: (0, 0), pipeline_mode=pl.Buffered(1))]
        mixer_args = [ret, att, w_mix]
    return pl.pallas_call(
        kern,
        out_shape=jax.ShapeDtypeStruct((bsz, t, d), F32),
        grid=(bsz, t // tm),
        in_specs=[pl.BlockSpec((None, tm, d), tok)] + mixer_specs + [
            pl.BlockSpec((None, N_MOD, d), lambda b, i: (b, 0, 0)),
            pl.BlockSpec((1, d), lambda b, i: (0, 0)),
            pl.BlockSpec((d, 2 * D_FF), lambda b, i: (0, 0), pipeline_mode=pl.Buffered(1)),
            pl.BlockSpec((D_FF, d), lambda b, i: (0, 0), pipeline_mode=pl.Buffered(1)),
            pl.BlockSpec((1, d), lambda b, i: (0, 0)),
        ],
        out_specs=pl.BlockSpec((None, tm, d), tok),
        scratch_shapes=[pltpu.VMEM((tm, d), BF16), pltpu.VMEM((tm, D_FF), BF16)],
        compiler_params=pltpu.CompilerParams(
            dimension_semantics=("arbitrary", "arbitrary"), vmem_limit_bytes=VMEM_LIMIT),
        name=f"ffn{sub // 2 + 1}",
    )(x, *mixer_args, mod, g, w_in, w_out, g_final)


def _store_head_pairs(val, out_ref):
    for g in range(val.shape[1] // LANES):
        out_ref[g] = val[:, g * LANES:(g + 1) * LANES].astype(BF16)


def _proj_kernel(x_ref, mod_ref, g_ref, w_ref, tr_ref, br_ref, ta_ref, ba_ref,
                 rq_ref, rk_ref, rv_ref, rg_ref, aq_ref, ak_ref, av_ref, h_ref):
    x = x_ref[...]
    h_ref[...] = _norm_mod(x, g_ref[...], mod_ref[3:4, :], mod_ref[4:5, :]).astype(BF16)

    def proj(c):
        return jnp.dot(h_ref[...], w_ref[:, c * 512:(c + 1) * 512], preferred_element_type=F32)

    def rope_ret(p, t_cos, t_sin):
        cb, sb, cbs, sbs = br_ref[0:1, :], br_ref[1:2, :], br_ref[2:3, :], br_ref[3:4, :]
        cos = t_cos * cb - t_sin * sb
        sin = t_cos * sbs + t_sin * cbs
        outs = []
        for hh in range(RET_HEADS):
            cols = slice(hh * LANES, (hh + 1) * LANES)
            xh = p[:, cols]
            outs.append(xh * cos[:, cols] + pltpu.roll(xh, RET_HEAD_DIM // 2, 1) * sin[:, cols])
        return jnp.concatenate(outs, axis=1)

    def rope_attn(p):
        t_cos, t_sin = ta_ref[0], ta_ref[1]
        cos = t_cos * ba_ref[0:1, :] - t_sin * ba_ref[1:2, :]
        sin_lo = t_cos * ba_ref[3:4, :] + t_sin * ba_ref[2:3, :]
        sin_hi = t_cos * ba_ref[5:6, :] + t_sin * ba_ref[4:5, :]
        outs = []
        for hh in range(ATTN_WIDTH // LANES):
            xh = p[:, hh * LANES:(hh + 1) * LANES]
            outs.append(xh * cos + pltpu.roll(xh, 96, 1) * sin_lo + pltpu.roll(xh, 32, 1) * sin_hi)
        return jnp.concatenate(outs, axis=1)

    rq_ref[...] = rope_ret(proj(0), tr_ref[0], tr_ref[1]).astype(BF16)
    rk_ref[...] = rope_ret(proj(1), tr_ref[2], tr_ref[3]).astype(BF16)
    _store_head_pairs(rope_attn(proj(4)) * (LOG2_E * ATTN_HEAD_DIM ** -0.5), aq_ref)
    _store_head_pairs(rope_attn(proj(5)), ak_ref)
    _store_head_pairs(proj(6), av_ref)
    rv_ref[...] = proj(2).astype(BF16)
    rg_ref[...] = proj(3).astype(BF16)


def _rope_tables(t, tm):
    assert tm % RET_STEP == 0
    p = jnp.arange(tm, dtype=F32)[:, None]
    base = (jnp.arange(t // tm, dtype=F32) * tm)[:, None]
    ones = jnp.ones((RET_HEAD_DIM // 2,), F32)
    inv_r = ROPE_THETA ** (-jnp.arange(0, RET_HEAD_DIM, 2, dtype=F32) / RET_HEAD_DIM)
    f_r = jnp.concatenate([inv_r, inv_r])
    cos_p, sin_p = jnp.cos(p * f_r), jnp.sin(p * f_r)
    loc = p % RET_STEP
    log_g = [math.log1p(-(2.0 ** (-5.0 - hh))) for hh in range(RET_HEADS)]
    dq = [jnp.exp(loc * lg) for lg in log_g]
    dk = [jnp.exp(-loc * lg) * RET_HEAD_DIM ** -0.5 for lg in log_g]
    per_head = lambda scales, tab: jnp.concatenate([s * tab for s in scales], axis=1)
    tab_r = jnp.stack([per_head(dq, cos_p), per_head(dq, sin_p), per_head(dk, cos_p), per_head(dk, sin_p)])
    sign = jnp.concatenate([-ones, ones])
    cos_b, sin_b = jnp.cos(base * f_r), jnp.sin(base * f_r)
    heads = lambda a: jnp.concatenate([a] * RET_HEADS, axis=1)
    base_r = jnp.stack([heads(cos_b), heads(sin_b), heads(sign * cos_b), heads(sign * sin_b)], axis=1)
    inv_a = ROPE_THETA ** (-jnp.arange(0, ATTN_HEAD_DIM, 2, dtype=F32) / ATTN_HEAD_DIM)
    f_a = jnp.concatenate([inv_a] * 4)
    tab_a = jnp.stack([jnp.cos(p * f_a), jnp.sin(p * f_a)])
    half = jnp.ones((ATTN_HEAD_DIM // 2,), F32)
    pat_lo = jnp.concatenate([-half, 0 * half, -half, 0 * half])
    pat_hi = jnp.concatenate([0 * half, half, 0 * half, half])
    cos_ba, sin_ba = jnp.cos(base * f_a), jnp.sin(base * f_a)
    zero = jnp.zeros_like(cos_ba)
    base_a = jnp.stack([cos_ba, sin_ba, pat_lo * cos_ba, pat_lo * sin_ba, pat_hi * cos_ba, pat_hi * sin_ba,
                        zero, zero], axis=1)
    return tab_r, base_r, tab_a, base_a


def _mixer_in_proj(x, mod, g, w):
    bsz, t, d = x.shape
    tm = PROJ_TM
    tab_r, base_r, tab_a, base_a = _rope_tables(t, tm)
    tok = lambda b, i: (b, i, 0)
    const3 = lambda b, i: (0, 0, 0)
    pairs = ATTN_WIDTH // LANES
    ret = jax.ShapeDtypeStruct((bsz, t, RET_WIDTH), BF16)
    att = jax.ShapeDtypeStruct((bsz, pairs, t, LANES), BF16)
    ret_spec = pl.BlockSpec((None, tm, RET_WIDTH), tok)
    att_spec = pl.BlockSpec((None, pairs, tm, LANES), lambda b, i: (b, 0, i, 0))
    return pl.pallas_call(
        _proj_kernel,
        out_shape=[ret] * 4 + [att] * 3,
        grid=(bsz, t // tm),
        in_specs=[
            pl.BlockSpec((None, tm, d), tok),
            pl.BlockSpec((None, N_MOD, d), lambda b, i: (b, 0, 0)),
            pl.BlockSpec((1, d), lambda b, i: (0, 0)),
            pl.BlockSpec((d, IN_COLS), lambda b, i: (0, 0), pipeline_mode=pl.Buffered(1)),
            pl.BlockSpec(tab_r.shape, const3, pipeline_mode=pl.Buffered(1)),
            pl.BlockSpec((None,) + base_r.shape[1:], lambda b, i: (i, 0, 0)),
            pl.BlockSpec(tab_a.shape, const3, pipeline_mode=pl.Buffered(1)),
            pl.BlockSpec((None,) + base_a.shape[1:], lambda b, i: (i, 0, 0)),
        ],
        out_specs=[ret_spec] * 4 + [att_spec] * 3,
        scratch_shapes=[pltpu.VMEM((tm, d), BF16)],
        compiler_params=pltpu.CompilerParams(
            dimension_semantics=("arbitrary", "arbitrary"), vmem_limit_bytes=VMEM_LIMIT),
        name="mixer_in_proj",
    )(x, mod, g, w, tab_r, base_r, tab_a, base_a)


def _block_diag(a, b):
    za, zb = jnp.zeros_like(a), jnp.zeros_like(b)
    return jnp.concatenate([jnp.concatenate([a, zb], axis=1), jnp.concatenate([za, b], axis=1)], axis=0)


def _ret_kernel(q_ref, k_ref, v_ref, g_ref, gn_ref, o_ref, state_ref, tri_ref):
    c, d = RET_STEP, RET_HEAD_DIM
    log_g = [math.log1p(-(2.0 ** (-5.0 - hh))) for hh in range(RET_HEADS)]

    @pl.when(jnp.logical_and(pl.program_id(0) == 0, pl.program_id(1) == 0))
    def _():
        ii = lax.broadcasted_iota(jnp.int32, (c, c), 0)
        jj = lax.broadcasted_iota(jnp.int32, (c, c), 1)
        tri_ref[...] = jnp.where(ii >= jj, 1.0, 0.0)

    @pl.when(pl.program_id(1) == 0)
    def _():
        state_ref[...] = jnp.zeros_like(state_ref)

    pairs = RET_HEADS // 2
    steps = RET_TC // c
    contract_rows = (((0,), (0,)), ((), ()))
    kv = {}
    for hp in range(pairs):
        cols = slice(2 * hp * d, 2 * (hp + 1) * d)
        for n in range(steps):
            rows = slice(n * c, (n + 1) * c)
            kv2 = lax.dot_general(k_ref[rows, cols], v_ref[rows, cols], contract_rows,
                                  preferred_element_type=F32)
            kv[hp, n] = (kv2[:d, :d], kv2[d:, d:])
    seen = {}
    for hp in range(pairs):
        for hh in range(2):
            h = 2 * hp + hh
            st = state_ref[h]
            for n in range(steps):
                seen[h, n] = (math.exp(log_g[h]) * st).astype(BF16)
                st = math.exp(c * log_g[h]) * st + math.exp((c - 1) * log_g[h]) * kv[hp, n][hh]
            state_ref[h] = st
    tri2 = jnp.concatenate([tri_ref[...], tri_ref[...]], axis=1)
    for hp in range(pairs):
        cols = slice(2 * hp * d, 2 * (hp + 1) * d)
        for n in range(steps):
            rows = slice(n * c, (n + 1) * c)
            q2, k2, v2 = q_ref[rows, cols], k_ref[rows, cols], v_ref[rows, cols]
            kbd = _block_diag(k2[:, :d], k2[:, d:])
            s2 = lax.dot_general(q2, kbd, (((1,), (1,)), ((), ())), preferred_element_type=F32) * tri2
            inner = jnp.dot(s2.astype(BF16), _block_diag(v2[:, :d], v2[:, d:]), preferred_element_type=F32)
            cross = jnp.dot(q2, _block_diag(seen[2 * hp, n], seen[2 * hp + 1, n]), preferred_element_type=F32)
            o2 = inner + cross
            for hh in range(2):
                hc = slice((2 * hp + hh) * d, (2 * hp + hh + 1) * d)
                o = o2[:, hh * d:(hh + 1) * d]
                mu = jnp.mean(o, axis=-1, keepdims=True)
                oc = o - mu
                var = jnp.mean(oc * oc, axis=-1, keepdims=True)
                gt = g_ref[rows, hc].astype(F32)
                o_ref[rows, hc] = (oc * lax.rsqrt(var + EPS) * gn_ref[:, hc] * (gt * _sigmoid(gt))).astype(BF16)


def _retention(rq, rk, rv, rg, gn):
    bsz, t, w = rq.shape
    tok = pl.BlockSpec((None, RET_TC, w), lambda b, i: (b, i, 0))
    return pl.pallas_call(
        _ret_kernel,
        out_shape=jax.ShapeDtypeStruct((bsz, t, w), BF16),
        grid=(bsz, t // RET_TC),
        in_specs=[tok, tok, tok, tok, pl.BlockSpec((1, w), lambda b, i: (0, 0))],
        out_specs=tok,
        scratch_shapes=[
            pltpu.VMEM((RET_HEADS, RET_HEAD_DIM, RET_HEAD_DIM), F32),
            pltpu.VMEM((RET_STEP, RET_STEP), F32),
        ],
        compiler_params=pltpu.CompilerParams(
            dimension_semantics=("arbitrary", "arbitrary"), vmem_limit_bytes=VMEM_LIMIT),
        name="retention",
    )(rq, rk, rv, rg, gn)


def _attn_unit(q2, k2, v2, bias, lo):
    blk = ATTN_BLOCK
    zero = jnp.zeros_like(q2)
    qs = jnp.concatenate([jnp.where(lo, q2, zero), jnp.where(lo, zero, q2)], axis=0)
    sc = lax.dot_general(qs, k2, (((1,), (1,)), ((), ())), preferred_element_type=F32)
    sc = sc + jnp.concatenate([bias, bias], axis=0)
    m = jnp.max(sc, axis=-1, keepdims=True)
    p = jnp.exp2(sc - m).astype(BF16)
    va = jnp.concatenate([v2, jnp.ones_like(v2)], axis=1)
    oa = jnp.dot(p, va, preferred_element_type=F32)
    o = jnp.where(lo, oa[:blk, :LANES], oa[blk:, :LANES])
    l = jnp.where(lo, oa[:blk, LANES:], oa[blk:, LANES:])
    mm = jnp.where(lo, jnp.broadcast_to(m[:blk], (blk, LANES)), jnp.broadcast_to(m[blk:], (blk, LANES)))
    return o / l, mm + jnp.log2(l)


def _attn_kernel(q1_ref, q4_ref, q16_ref,
                 k1p_ref, k1c_ref, k4p_ref, k4c_ref, k16p_ref, k16c_ref,
                 v1p_ref, v1c_ref, v4p_ref, v4c_ref, v16p_ref, v16c_ref,
                 g_ref, o_ref, bias_ref, ob_ref, lb_ref):
    tt, blk = ATTN_TT, ATTN_BLOCK
    tile = pl.program_id(2)
    lane = lax.broadcasted_iota(jnp.int32, (blk, LANES), 1)
    lo = lane < ATTN_HEAD_DIM
    qi = lax.broadcasted_iota(jnp.int32, (blk, 2 * blk), 0)
    kj = lax.broadcasted_iota(jnp.int32, (blk, 2 * blk), 1)
    neg = jnp.float32(-jnp.inf)
    band = jnp.where((kj >= qi) & (kj <= qi + blk), 0.0, neg)
    bias_ref[0] = band
    bias_ref[1] = jnp.where(jnp.logical_or(kj >= blk, tile > 0), band, neg)

    q_refs = (q1_ref, q4_ref, q16_ref)
    k_refs = ((k1p_ref, k1c_ref), (k4p_ref, k4c_ref), (k16p_ref, k16c_ref))
    v_refs = ((v1p_ref, v1c_ref), (v4p_ref, v4c_ref), (v16p_ref, v16c_ref))

    def rows_of(ref, r, s, lo_row, n):
        return ref[lo_row:lo_row + n, s * LANES:(s + 1) * LANES]

    def keys(refs, r, s, j):
        prev_ref, cur_ref = refs
        if j == 0:
            return jnp.concatenate([rows_of(prev_ref, r, s, 0, blk), rows_of(cur_ref, r, s, 0, blk)], axis=0)
        return rows_of(cur_ref, r, s, (j - 1) * blk, 2 * blk)

    for bi, (window, r) in enumerate(DILATED_PATTERNS):
        assert window // r == blk
        for s in range(r):
            for j in range(tt // (blk * r)):
                q2 = rows_of(q_refs[bi], r, s, j * blk, blk)
                k2 = keys(k_refs[bi], r, s, j)
                v2 = keys(v_refs[bi], r, s, j)
                o, lse = _attn_unit(q2, k2, v2, bias_ref[1 if j == 0 else 0], lo)
                start = s + r * blk * j
                rows = pl.ds(start, blk, stride=r) if r > 1 else pl.ds(start, blk)
                ob_ref[bi, rows, :] = o
                lb_ref[bi, rows, :] = lse

    rc = 256
    lane_c = lax.broadcasted_iota(jnp.int32, (rc, LANES), 1)
    lo_c = lane_c < ATTN_HEAD_DIM
    gain = g_ref[...]

    for ci in range(tt // rc):
        rows = slice(ci * rc, (ci + 1) * rc)
        l0, l1, l2 = lb_ref[0, rows, :], lb_ref[1, rows, :], lb_ref[2, rows, :]
        mx = jnp.maximum(jnp.maximum(l0, l1), l2)
        e0, e1, e2 = jnp.exp2(l0 - mx), jnp.exp2(l1 - mx), jnp.exp2(l2 - mx)
        att = (e0 * ob_ref[0, rows, :] + e1 * ob_ref[1, rows, :] + e2 * ob_ref[2, rows, :]) / (e0 + e1 + e2)
        sq = att * att
        ms_lo = jnp.sum(jnp.where(lo_c, sq, 0.0), axis=-1, keepdims=True)
        ms_hi = jnp.sum(jnp.where(lo_c, 0.0, sq), axis=-1, keepdims=True)
        ms = jnp.where(lo_c, ms_lo, ms_hi) * (1.0 / ATTN_HEAD_DIM)
        o_ref[rows, :] = (att * lax.rsqrt(ms + EPS) * gain).astype(BF16)


def _dilated_attention(aq, ak, av, gain):
    bsz, ngrp, t, _ = aq.shape
    tt, blk = ATTN_TT, ATTN_BLOCK
    w = ngrp * LANES

    def cur_prev(r):
        rows = tt // r
        cur = pl.BlockSpec((None, None, rows, r * LANES), lambda b, hp, i: (b, hp, i, 0))
        prev = pl.BlockSpec((None, None, blk, r * LANES),
                            lambda b, hp, i: (b, hp, jnp.maximum(i * (rows // blk) - 1, 0), 0))
        return cur, prev

    view = lambda a, r: a.reshape(bsz, ngrp, t // r, r * LANES)
    specs = [cur_prev(r) for _, r in DILATED_PATTERNS]
    q_specs = [c for c, _ in specs]
    kv_specs = [sp for c, p in specs for sp in (p, c)]
    kv_args = lambda a: [view(a, r) for _, r in DILATED_PATTERNS for _ in range(2)]
    aq = [view(aq, r) for _, r in DILATED_PATTERNS]
    return pl.pallas_call(
        _attn_kernel,
        out_shape=jax.ShapeDtypeStruct((bsz, t, w), BF16),
        grid=(bsz, ngrp, t // tt),
        in_specs=q_specs + kv_specs + kv_specs + [pl.BlockSpec((1, LANES), lambda b, hp, i: (0, hp))],
        out_specs=pl.BlockSpec((None, tt, LANES), lambda b, hp, i: (b, i, hp)),
        scratch_shapes=[
            pltpu.VMEM((2, blk, 2 * blk), F32),
            pltpu.VMEM((3, tt, LANES), F32), pltpu.VMEM((3, tt, LANES), F32),
        ],
        compiler_params=pltpu.CompilerParams(
            dimension_semantics=("arbitrary", "arbitrary", "arbitrary"), vmem_limit_bytes=VMEM_LIMIT),
        name="dilated_attention",
    )(*aq, *kv_args(ak), *kv_args(av), gain)


def kernel(x, c, w_ada, b_ada, norm1_g, ffn1_w_in, ffn1_w_out, norm_mix_g, w_in_mix, ret_gn_g, attn_norm_g,
           w_out_mix, norm2_g, ffn2_w_in, ffn2_w_out, norm_f_g):
    depth = w_ada.shape[0]
    assert depth >= 1, "the final RMSNorm is fused into the last layer's second FFN"
    bsz, t, d = x.shape
    gf = norm_f_g.reshape(1, d)
    for l in range(depth):
        last = l == depth - 1
        mod = _adaln_mod(c, w_ada[l], b_ada[l]).reshape(bsz, N_MOD, d)
        x = _ffn(x, mod, norm1_g[l].reshape(1, d), ffn1_w_in[l].astype(BF16), ffn1_w_out[l].astype(BF16), gf,
                 sub=0, final_norm=False)
        proj = _mixer_in_proj(x, mod, norm_mix_g[l].reshape(1, d), w_in_mix[l].astype(BF16))
        rq, rk, rv, rg = proj[:4]
        ret = _retention(rq, rk, rv, rg, ret_gn_g[l].reshape(1, RET_WIDTH))
        att = _dilated_attention(proj[4], proj[5], proj[6], attn_norm_g[l].reshape(1, ATTN_WIDTH))
        x = _ffn(x, mod, norm2_g[l].reshape(1, d), ffn2_w_in[l].astype(BF16), ffn2_w_out[l].astype(BF16), gf,
                 sub=2, final_norm=last, mixer=(ret, att, w_out_mix[l].astype(BF16)))
    return x
```

```python
import functools
import math

import jax
import jax.numpy as jnp
from jax import lax
from jax.experimental import pallas as pl
from jax.experimental.pallas import tpu as pltpu

D_MODEL = 1024
RET_WIDTH = 512
RET_HEAD_DIM = 128
RET_HEADS = 4
RET_CHUNK = 128
ATTN_WIDTH = 512
ATTN_HEAD_DIM = 64
DILATED_PATTERNS = ((128, 1), (512, 4), (2048, 16))
ATTN_BLOCK = 128
ROPE_THETA = 10000.0
D_FF = 2816
N_MOD = 9
IN_COLS = 4 * RET_WIDTH + 3 * ATTN_WIDTH
EPS = 1e-6
LOG2_E = math.log2(math.e)

LANES = 128
VMEM_LIMIT = 56 * 1024 * 1024

FFN_TM = 1024
FFN_FC = 256
PROJ_TM = 512
RET_TC = 2048
RET_STEP = 256
ATTN_TT = ATTN_BLOCK * DILATED_PATTERNS[-1][1]
BF16 = jnp.bfloat16
F32 = jnp.float32


def _sigmoid(x):
    return 1.0 / (1.0 + jnp.exp(-x))


def _norm_mod(x, g, shift, scale):
    ms = jnp.mean(x * x, axis=-1, keepdims=True)
    return (x * lax.rsqrt(ms + EPS) * g) * (1.0 + scale) + shift


def _mod_kernel(c_ref, w_ref, b_ref, o_ref):
    c = c_ref[...]
    ca = c * _sigmoid(c)
    o_ref[...] = jnp.dot(ca, w_ref[...], preferred_element_type=F32) + b_ref[...]


def _adaln_mod(c, w, b):
    bsz, d = c.shape
    n = w.shape[1]
    tn = 1024
    return pl.pallas_call(
        _mod_kernel,
        out_shape=jax.ShapeDtypeStruct((bsz, n), F32),
        grid=(n // tn,),
        in_specs=[
            pl.BlockSpec((bsz, d), lambda j: (0, 0)),
            pl.BlockSpec((d, tn), lambda j: (0, j)),
            pl.BlockSpec((1, tn), lambda j: (0, j)),
        ],
        out_specs=pl.BlockSpec((bsz, tn), lambda j: (0, j)),
        compiler_params=pltpu.CompilerParams(dimension_semantics=("arbitrary",)),
        name="adaln_mod",
    )(c, w, b.reshape(1, n))


def _ffn_kernel(*refs, sub, mixer_out, final_norm, n_cast):
    if mixer_out:
        x_ref, ret_ref, att_ref, wmix_ref, *refs = refs
    else:
        x_ref, *refs = refs
    mod_ref, g_ref, win_ref, wout_ref, gf_ref, *refs = refs
    cast_in, (o_ref, *refs) = refs[:n_cast], refs[n_cast:]
    cast_out, (h_ref, act_ref) = refs[:n_cast], refs[n_cast:]
    for src, dst in zip(cast_in, cast_out):
        dst[...] = src[...].astype(BF16)
    x = x_ref[...]
    if mixer_out:
        y = jnp.dot(ret_ref[...], wmix_ref[0:RET_WIDTH, :], preferred_element_type=F32)
        y = y + jnp.dot(att_ref[...], wmix_ref[RET_WIDTH:, :], preferred_element_type=F32)
        x = x + mod_ref[5:6, :] * y
    shift = mod_ref[3 * sub:3 * sub + 1, :]
    scale = mod_ref[3 * sub + 1:3 * sub + 2, :]
    gate = mod_ref[3 * sub + 2:3 * sub + 3, :]
    h_ref[...] = _norm_mod(x, g_ref[...], shift, scale).astype(BF16)
    for j in range(D_FF // FFN_FC):
        h = h_ref[...]
        a = jnp.dot(h, win_ref[:, j * FFN_FC:(j + 1) * FFN_FC], preferred_element_type=F32)
        b = jnp.dot(h, win_ref[:, D_FF + j * FFN_FC:D_FF + (j + 1) * FFN_FC], preferred_element_type=F32)
        act_ref[:, j * FFN_FC:(j + 1) * FFN_FC] = (a * _sigmoid(a) * b).astype(BF16)
    y = jnp.dot(act_ref[...], wout_ref[...], preferred_element_type=F32)
    xn = x + 0.5 * gate * y
    if final_norm:
        ms = jnp.mean(xn * xn, axis=-1, keepdims=True)
        xn = xn * lax.rsqrt(ms + EPS) * gf_ref[...]
    o_ref[...] = xn


def _ffn(x, mod, g, w_in, w_out, g_final, *, sub, final_norm, mixer=None, cast=()):
    bsz, t, d = x.shape
    tm = FFN_TM
    n_i = t // tm
    kern = functools.partial(_ffn_kernel, sub=sub, mixer_out=mixer is not None, final_norm=final_norm,
                             n_cast=len(cast))
    tok = lambda b, i: (b, i, 0)
    cast_specs = []
    for a in cast:
        chunks = next(c for c in (bsz * n_i // f for f in range(1, bsz * n_i + 1) if (bsz * n_i) % f == 0)
                      if a.shape[0] % c == 0 and (a.shape[0] // c) % 16 == 0)
        group = bsz * n_i // chunks
        cast_specs.append(pl.BlockSpec((a.shape[0] // chunks, a.shape[1]),
                                       lambda b, i, group=group: ((b * n_i + i) // group, 0)))
    mixer_specs, mixer_args = [], []
    if mixer is not None:
        ret, att, w_mix = mixer
        mixer_specs = [pl.BlockSpec((None, tm, ret.shape[-1]), tok), pl.BlockSpec((None, tm, att.shape[-1]), tok),
                       pl.BlockSpec(w_mix.shape, lambda b, i: (0, 0), pipeline_mode=pl.Buffered(1))]
        mixer_args = [ret, att, w_mix]
    out, *casted = pl.pallas_call(
        kern,
        out_shape=[jax.ShapeDtypeStruct((bsz, t, d), F32)] + [jax.ShapeDtypeStruct(a.shape, BF16) for a in cast],
        grid=(bsz, n_i),
        in_specs=[pl.BlockSpec((None, tm, d), tok)] + mixer_specs + [
            pl.BlockSpec((None, N_MOD, d), lambda b, i: (b, 0, 0)),
            pl.BlockSpec((1, d), lambda b, i: (0, 0)),
            pl.BlockSpec((d, 2 * D_FF), lambda b, i: (0, 0), pipeline_mode=pl.Buffered(1)),
            pl.BlockSpec((D_FF, d), lambda b, i: (0, 0), pipeline_mode=pl.Buffered(1)),
            pl.BlockSpec((1, d), lambda b, i: (0, 0)),
        ] + cast_specs,
        out_specs=[pl.BlockSpec((None, tm, d), tok)] + cast_specs,
        scratch_shapes=[pltpu.VMEM((tm, d), BF16), pltpu.VMEM((tm, D_FF), BF16)],
        compiler_params=pltpu.CompilerParams(
            dimension_semantics=("arbitrary", "arbitrary"), vmem_limit_bytes=VMEM_LIMIT),
        name=f"ffn{sub // 2 + 1}",
    )(x, *mixer_args, mod, g, w_in, w_out, g_final, *cast)
    return (out, casted) if cast else out


def _store_residue_layouts(val, nat_ref, p4_ref, p16_ref, t1_ref, t4_ref):
    tm = val.shape[0]
    nat_ref[...] = val.astype(BF16)
    for g in range(val.shape[1] // LANES):
        cols = slice(g * LANES, (g + 1) * LANES)
        t1_ref[g] = val[:, cols]
        for s4 in range(4):
            y = t1_ref[g, pl.ds(s4, tm // 4, stride=4), :]
            p4_ref[s4, :, cols] = y.astype(BF16)
            t4_ref[g, s4 * (tm // 4):(s4 + 1) * (tm // 4), :] = y
        for s16 in range(16):
            s4, u = s16 % 4, s16 // 4
            y = t4_ref[g, pl.ds(s4 * (tm // 4) + u, tm // 16, stride=4), :]
            p16_ref[s16, :, cols] = y.astype(BF16)


def _proj_kernel(x_ref, mod_ref, g_ref, w_ref, tr_ref, br_ref, ta_ref, ba_ref,
                 rq_ref, rk_ref, rv_ref, rg_ref,
                 aq1_ref, aq4_ref, aq16_ref, ak1_ref, ak4_ref, ak16_ref, av1_ref, av4_ref, av16_ref,
                 h_ref, t1_ref, t4_ref):
    x = x_ref[...]
    h_ref[...] = _norm_mod(x, g_ref[...], mod_ref[3:4, :], mod_ref[4:5, :]).astype(BF16)

    def proj(c):
        return jnp.dot(h_ref[...], w_ref[:, c * 512:(c + 1) * 512], preferred_element_type=F32)

    def rope_ret(p, t_cos, t_sin):
        cb, sb, cbs, sbs = br_ref[0:1, :], br_ref[1:2, :], br_ref[2:3, :], br_ref[3:4, :]
        cos = t_cos * cb - t_sin * sb
        sin = t_cos * sbs + t_sin * cbs
        outs = []
        for hh in range(RET_HEADS):
            cols = slice(hh * LANES, (hh + 1) * LANES)
            xh = p[:, cols]
            outs.append(xh * cos[:, cols] + pltpu.roll(xh, RET_HEAD_DIM // 2, 1) * sin[:, cols])
        return jnp.concatenate(outs, axis=1)

    def rope_attn(p):
        t_cos, t_sin = ta_ref[0], ta_ref[1]
        cos = t_cos * ba_ref[0:1, :] - t_sin * ba_ref[1:2, :]
        sin_lo = t_cos * ba_ref[3:4, :] + t_sin * ba_ref[2:3, :]
        sin_hi = t_cos * ba_ref[5:6, :] + t_sin * ba_ref[4:5, :]
        outs = []
        for hh in range(ATTN_WIDTH // LANES):
            xh = p[:, hh * LANES:(hh + 1) * LANES]
            outs.append(xh * cos + pltpu.roll(xh, 96, 1) * sin_lo + pltpu.roll(xh, 32, 1) * sin_hi)
        return jnp.concatenate(outs, axis=1)

    rq_ref[...] = rope_ret(proj(0), tr_ref[0], tr_ref[1]).astype(BF16)
    rk_ref[...] = rope_ret(proj(1), tr_ref[2], tr_ref[3]).astype(BF16)
    rv_ref[...] = proj(2).astype(BF16)
    rg_ref[...] = proj(3).astype(BF16)
    _store_residue_layouts(rope_attn(proj(4)) * (LOG2_E * ATTN_HEAD_DIM ** -0.5),
                           aq1_ref, aq4_ref, aq16_ref, t1_ref, t4_ref)
    _store_residue_layouts(rope_attn(proj(5)), ak1_ref, ak4_ref, ak16_ref, t1_ref, t4_ref)
    _store_residue_layouts(proj(6), av1_ref, av4_ref, av16_ref, t1_ref, t4_ref)


def _rope_tables(t, tm):
    assert tm % RET_STEP == 0
    p = jnp.arange(tm, dtype=F32)[:, None]
    base = (jnp.arange(t // tm, dtype=F32) * tm)[:, None]
    ones = jnp.ones((RET_HEAD_DIM // 2,), F32)
    inv_r = ROPE_THETA ** (-jnp.arange(0, RET_HEAD_DIM, 2, dtype=F32) / RET_HEAD_DIM)
    f_r = jnp.concatenate([inv_r, inv_r])
    cos_p, sin_p = jnp.cos(p * f_r), jnp.sin(p * f_r)
    loc = p % RET_STEP
    log_g = [math.log1p(-(2.0 ** (-5.0 - hh))) for hh in range(RET_HEADS)]
    dq = [jnp.exp(loc * lg) for lg in log_g]
    dk = [jnp.exp(-loc * lg) * RET_HEAD_DIM ** -0.5 for lg in log_g]
    per_head = lambda scales, tab: jnp.concatenate([s * tab for s in scales], axis=1)
    tab_r = jnp.stack([per_head(dq, cos_p), per_head(dq, sin_p), per_head(dk, cos_p), per_head(dk, sin_p)])
    sign = jnp.concatenate([-ones, ones])
    cos_b, sin_b = jnp.cos(base * f_r), jnp.sin(base * f_r)
    heads = lambda a: jnp.concatenate([a] * RET_HEADS, axis=1)
    base_r = jnp.stack([heads(cos_b), heads(sin_b), heads(sign * cos_b), heads(sign * sin_b)], axis=1)
    inv_a = ROPE_THETA ** (-jnp.arange(0, ATTN_HEAD_DIM, 2, dtype=F32) / ATTN_HEAD_DIM)
    f_a = jnp.concatenate([inv_a] * 4)
    tab_a = jnp.stack([jnp.cos(p * f_a), jnp.sin(p * f_a)])
    half = jnp.ones((ATTN_HEAD_DIM // 2,), F32)
    pat_lo = jnp.concatenate([-half, 0 * half, -half, 0 * half])
    pat_hi = jnp.concatenate([0 * half, half, 0 * half, half])
    cos_ba, sin_ba = jnp.cos(base * f_a), jnp.sin(base * f_a)
    zero = jnp.zeros_like(cos_ba)
    base_a = jnp.stack([cos_ba, sin_ba, pat_lo * cos_ba, pat_lo * sin_ba, pat_hi * cos_ba, pat_hi * sin_ba,
                        zero, zero], axis=1)
    return tab_r, base_r, tab_a, base_a


def _mixer_in_proj(x, mod, g, w):
    bsz, t, d = x.shape
    tm = PROJ_TM
    tab_r, base_r, tab_a, base_a = _rope_tables(t, tm)
    tok = lambda b, i: (b, i, 0)
    const3 = lambda b, i: (0, 0, 0)
    aw = ATTN_WIDTH
    nat = jax.ShapeDtypeStruct((bsz, t, aw), BF16)
    p4 = jax.ShapeDtypeStruct((bsz, 4, t // 4, aw), BF16)
    p16 = jax.ShapeDtypeStruct((bsz, 16, t // 16, aw), BF16)
    nat_spec = pl.BlockSpec((None, tm, aw), tok)
    p4_spec = pl.BlockSpec((None, 4, tm // 4, aw), lambda b, i: (b, 0, i, 0))
    p16_spec = pl.BlockSpec((None, 16, tm // 16, aw), lambda b, i: (b, 0, i, 0))
    return pl.pallas_call(
        _proj_kernel,
        out_shape=[nat] * 4 + [nat, p4, p16] * 3,
        grid=(bsz, t // tm),
        in_specs=[
            pl.BlockSpec((None, tm, d), tok),
            pl.BlockSpec((None, N_MOD, d), lambda b, i: (b, 0, 0)),
            pl.BlockSpec((1, d), lambda b, i: (0, 0)),
            pl.BlockSpec((d, IN_COLS), lambda b, i: (0, 0), pipeline_mode=pl.Buffered(1)),
            pl.BlockSpec(tab_r.shape, const3, pipeline_mode=pl.Buffered(1)),
            pl.BlockSpec((None,) + base_r.shape[1:], lambda b, i: (i, 0, 0)),
            pl.BlockSpec(tab_a.shape, const3, pipeline_mode=pl.Buffered(1)),
            pl.BlockSpec((None,) + base_a.shape[1:], lambda b, i: (i, 0, 0)),
        ],
        out_specs=[nat_spec] * 4 + [nat_spec, p4_spec, p16_spec] * 3,
        scratch_shapes=[pltpu.VMEM((tm, d), BF16),
                        pltpu.VMEM((aw // LANES, tm, LANES), F32), pltpu.VMEM((aw // LANES, tm, LANES), F32)],
        compiler_params=pltpu.CompilerParams(
            dimension_semantics=("arbitrary", "arbitrary"), vmem_limit_bytes=VMEM_LIMIT),
        name="mixer_in_proj",
    )(x, mod, g, w, tab_r, base_r, tab_a, base_a)


def _block_diag(a, b):
    za, zb = jnp.zeros_like(a), jnp.zeros_like(b)
    return jnp.concatenate([jnp.concatenate([a, zb], axis=1), jnp.concatenate([za, b], axis=1)], axis=0)


def _ret_kernel(q_ref, k_ref, v_ref, g_ref, gn_ref, o_ref, state_ref, tri_ref):
    c, d = RET_STEP, RET_HEAD_DIM
    log_g = [math.log1p(-(2.0 ** (-5.0 - hh))) for hh in range(RET_HEADS)]

    @pl.when(jnp.logical_and(pl.program_id(0) == 0, pl.program_id(1) == 0))
    def _():
        ii = lax.broadcasted_iota(jnp.int32, (c, c), 0)
        jj = lax.broadcasted_iota(jnp.int32, (c, c), 1)
        tri_ref[...] = jnp.where(ii >= jj, 1.0, 0.0)

    @pl.when(pl.program_id(1) == 0)
    def _():
        state_ref[...] = jnp.zeros_like(state_ref)

    pairs = RET_HEADS // 2
    steps = RET_TC // c
    contract_rows = (((0,), (0,)), ((), ()))
    kv = {}
    for hp in range(pairs):
        cols = slice(2 * hp * d, 2 * (hp + 1) * d)
        for n in range(steps):
            rows = slice(n * c, (n + 1) * c)
            kv2 = lax.dot_general(k_ref[rows, cols], v_ref[rows, cols], contract_rows,
                                  preferred_element_type=F32)
            kv[hp, n] = (kv2[:d, :d], kv2[d:, d:])
    seen = {}
    for hp in range(pairs):
        for hh in range(2):
            h = 2 * hp + hh
            st = state_ref[h]
            for n in range(steps):
                seen[h, n] = (math.exp(log_g[h]) * st).astype(BF16)
                st = math.exp(c * log_g[h]) * st + math.exp((c - 1) * log_g[h]) * kv[hp, n][hh]
            state_ref[h] = st
    tri2 = jnp.concatenate([tri_ref[...], tri_ref[...]], axis=1)
    for hp in range(pairs):
        cols = slice(2 * hp * d, 2 * (hp + 1) * d)
        for n in range(steps):
            rows = slice(n * c, (n + 1) * c)
            q2, k2, v2 = q_ref[rows, cols], k_ref[rows, cols], v_ref[rows, cols]
            kbd = _block_diag(k2[:, :d], k2[:, d:])
            s2 = lax.dot_general(q2, kbd, (((1,), (1,)), ((), ())), preferred_element_type=F32) * tri2
            inner = jnp.dot(s2.astype(BF16), _block_diag(v2[:, :d], v2[:, d:]), preferred_element_type=F32)
            cross = jnp.dot(q2, _block_diag(seen[2 * hp, n], seen[2 * hp + 1, n]), preferred_element_type=F32)
            o2 = inner + cross
            for hh in range(2):
                hc = slice((2 * hp + hh) * d, (2 * hp + hh + 1) * d)
                o = o2[:, hh * d:(hh + 1) * d]
                mu = jnp.mean(o, axis=-1, keepdims=True)
                oc = o - mu
                var = jnp.mean(oc * oc, axis=-1, keepdims=True)
                gt = g_ref[rows, hc].astype(F32)
                o_ref[rows, hc] = (oc * lax.rsqrt(var + EPS) * gn_ref[:, hc] * (gt * _sigmoid(gt))).astype(BF16)


def _retention(rq, rk, rv, rg, gn):
    bsz, t, w = rq.shape
    tok = pl.BlockSpec((None, RET_TC, w), lambda b, i: (b, i, 0))
    return pl.pallas_call(
        _ret_kernel,
        out_shape=jax.ShapeDtypeStruct((bsz, t, w), BF16),
        grid=(bsz, t // RET_TC),
        in_specs=[tok, tok, tok, tok, pl.BlockSpec((1, w), lambda b, i: (0, 0))],
        out_specs=tok,
        scratch_shapes=[
            pltpu.VMEM((RET_HEADS, RET_HEAD_DIM, RET_HEAD_DIM), F32),
            pltpu.VMEM((RET_STEP, RET_STEP), F32),
        ],
        compiler_params=pltpu.CompilerParams(
            dimension_semantics=("arbitrary", "arbitrary"), vmem_limit_bytes=VMEM_LIMIT),
        name="retention",
    )(rq, rk, rv, rg, gn)


def _attn_unit(q2, k2, v2, bias, lo):
    blk = ATTN_BLOCK
    zero = jnp.zeros_like(q2)
    qs = jnp.concatenate([jnp.where(lo, q2, zero), jnp.where(lo, zero, q2)], axis=0)
    sc = lax.dot_general(qs, k2, (((1,), (1,)), ((), ())), preferred_element_type=F32)
    sc = sc + jnp.concatenate([bias, bias], axis=0)
    m = jnp.max(sc, axis=-1, keepdims=True)
    p = jnp.exp2(sc - m).astype(BF16)
    va = jnp.concatenate([v2, jnp.ones_like(v2)], axis=1)
    oa = jnp.dot(p, va, preferred_element_type=F32)
    o = jnp.where(lo, oa[:blk, :LANES], oa[blk:, :LANES])
    l = jnp.where(lo, oa[:blk, LANES:], oa[blk:, LANES:])
    mm = jnp.where(lo, jnp.broadcast_to(m[:blk], (blk, LANES)), jnp.broadcast_to(m[blk:], (blk, LANES)))
    return o * (1.0 / l), mm + jnp.log2(l)


def _attn_kernel(q1_ref, q4_ref, q16_ref,
                 k1p_ref, k1c_ref, k4p_ref, k4c_ref, k16p_ref, k16c_ref,
                 v1p_ref, v1c_ref, v4p_ref, v4c_ref, v16p_ref, v16c_ref,
                 g_ref, o_ref, bias_ref, ob_ref, lb_ref, tb_ref):
    tt, blk = ATTN_TT, ATTN_BLOCK
    tile = pl.program_id(2)
    lane = lax.broadcasted_iota(jnp.int32, (blk, LANES), 1)
    lo = lane < ATTN_HEAD_DIM
    qi = lax.broadcasted_iota(jnp.int32, (blk, 2 * blk), 0)
    kj = lax.broadcasted_iota(jnp.int32, (blk, 2 * blk), 1)
    neg = jnp.float32(-jnp.inf)
    band = jnp.where((kj >= qi) & (kj <= qi + blk), 0.0, neg)
    bias_ref[0] = band
    bias_ref[1] = jnp.where(jnp.logical_or(kj >= blk, tile > 0), band, neg)

    q_refs = (q1_ref, q4_ref, q16_ref)
    k_refs = ((k1p_ref, k1c_ref), (k4p_ref, k4c_ref), (k16p_ref, k16c_ref))
    v_refs = ((v1p_ref, v1c_ref), (v4p_ref, v4c_ref), (v16p_ref, v16c_ref))

    def rows_of(ref, r, s, lo_row, n):
        return ref[lo_row:lo_row + n, :] if r == 1 else ref[s, lo_row:lo_row + n, :]

    def keys(refs, r, s, j):
        prev_ref, cur_ref = refs
        if j == 0:
            return jnp.concatenate([rows_of(prev_ref, r, s, 0, blk), rows_of(cur_ref, r, s, 0, blk)], axis=0)
        return rows_of(cur_ref, r, s, (j - 1) * blk, 2 * blk)

    for bi, (window, r) in enumerate(DILATED_PATTERNS):
        assert window // r == blk
        for s in range(r):
            for j in range(tt // (blk * r)):
                q2 = rows_of(q_refs[bi], r, s, j * blk, blk)
                k2 = keys(k_refs[bi], r, s, j)
                v2 = keys(v_refs[bi], r, s, j)
                o, lse = _attn_unit(q2, k2, v2, bias_ref[1 if j == 0 else 0], lo)
                if r == 16:
                    rows = pl.ds((s % 4) * (tt // 4) + s // 4, blk, stride=4)
                    tb_ref[0, rows, :] = o
                    tb_ref[1, rows, :] = lse
                else:
                    start = s + r * blk * j
                    rows = pl.ds(start, blk, stride=r) if r > 1 else pl.ds(start, blk)
                    ob_ref[bi, rows, :] = o
                    lb_ref[bi, rows, :] = lse
        if r == 16:
            for s4 in range(4):
                src = slice(s4 * (tt // 4), (s4 + 1) * (tt // 4))
                ob_ref[bi, pl.ds(s4, tt // 4, stride=4), :] = tb_ref[0, src, :]
                lb_ref[bi, pl.ds(s4, tt // 4, stride=4), :] = tb_ref[1, src, :]

    rc = 256
    li = lax.broadcasted_iota(jnp.int32, (LANES, LANES), 0) // ATTN_HEAD_DIM
    lj = lax.broadcasted_iota(jnp.int32, (LANES, LANES), 1) // ATTN_HEAD_DIM
    head_mean = jnp.where(li == lj, 1.0 / ATTN_HEAD_DIM, 0.0).astype(BF16)
    gain = g_ref[...]

    for ci in range(tt // rc):
        rows = slice(ci * rc, (ci + 1) * rc)
        l0, l1, l2 = lb_ref[0, rows, :], lb_ref[1, rows, :], lb_ref[2, rows, :]
        mx = jnp.maximum(jnp.maximum(l0, l1), l2)
        e0, e1, e2 = jnp.exp2(l0 - mx), jnp.exp2(l1 - mx), jnp.exp2(l2 - mx)
        att = (e0 * ob_ref[0, rows, :] + e1 * ob_ref[1, rows, :] + e2 * ob_ref[2, rows, :]) * (1.0 / (e0 + e1 + e2))
        ms = jnp.dot((att * att).astype(BF16), head_mean, preferred_element_type=F32)
        o_ref[rows, :] = (att * lax.rsqrt(ms + EPS) * gain).astype(BF16)


def _dilated_attention(aq, ak, av, gain):
    bsz, t, w = aq[0].shape
    tt, blk = ATTN_TT, ATTN_BLOCK
    ngrp = w // LANES

    def cur_prev(r):
        rows = tt // r
        if r == 1:
            cur = pl.BlockSpec((None, rows, LANES), lambda b, hp, i: (b, i, hp))
            prev = pl.BlockSpec((None, blk, LANES),
                                lambda b, hp, i: (b, jnp.maximum(i * (rows // blk) - 1, 0), hp))
        else:
            cur = pl.BlockSpec((None, r, rows, LANES), lambda b, hp, i: (b, 0, i, hp))
            prev = pl.BlockSpec((None, r, blk, LANES),
                                lambda b, hp, i: (b, 0, jnp.maximum(i * (rows // blk) - 1, 0), hp))
        return cur, prev

    specs = [cur_prev(r) for _, r in DILATED_PATTERNS]
    q_specs = [c for c, _ in specs]
    kv_specs = [sp for c, p in specs for sp in (p, c)]
    kv_args = lambda a: [x for arr in a for x in (arr, arr)]
    return pl.pallas_call(
        _attn_kernel,
        out_shape=jax.ShapeDtypeStruct((bsz, t, w), BF16),
        grid=(bsz, ngrp, t // tt),
        in_specs=q_specs + kv_specs + kv_specs + [pl.BlockSpec((1, LANES), lambda b, hp, i: (0, hp))],
        out_specs=pl.BlockSpec((None, tt, LANES), lambda b, hp, i: (b, i, hp)),
        scratch_shapes=[
            pltpu.VMEM((2, blk, 2 * blk), F32),
            pltpu.VMEM((3, tt, LANES), F32), pltpu.VMEM((3, tt, LANES), F32), pltpu.VMEM((2, tt, LANES), F32),
        ],
        compiler_params=pltpu.CompilerParams(
            dimension_semantics=("arbitrary", "arbitrary", "arbitrary"), vmem_limit_bytes=VMEM_LIMIT),
        name="dilated_attention",
    )(*aq, *kv_args(ak), *kv_args(av), gain)


def kernel(x, c, w_ada, b_ada, norm1_g, ffn1_w_in, ffn1_w_out, norm_mix_g, w_in_mix, ret_gn_g, attn_norm_g,
           w_out_mix, norm2_g, ffn2_w_in, ffn2_w_out, norm_f_g):
    depth = w_ada.shape[0]
    assert depth >= 1, "the final RMSNorm is fused into the last layer's second FFN"
    bsz, t, d = x.shape
    gf = norm_f_g.reshape(1, d)
    for l in range(depth):
        last = l == depth - 1
        mod = _adaln_mod(c, w_ada[l], b_ada[l]).reshape(bsz, N_MOD, d)
        x, (w_in_mix_b, w_out_mix_b, ffn2_w_in_b, ffn2_w_out_b) = _ffn(
            x, mod, norm1_g[l].reshape(1, d), ffn1_w_in[l].astype(BF16), ffn1_w_out[l].astype(BF16), gf,
            sub=0, final_norm=False, cast=(w_in_mix[l], w_out_mix[l], ffn2_w_in[l], ffn2_w_out[l]))
        proj = _mixer_in_proj(x, mod, norm_mix_g[l].reshape(1, d), w_in_mix_b)
        rq, rk, rv, rg = proj[:4]
        ret = _retention(rq, rk, rv, rg, ret_gn_g[l].reshape(1, RET_WIDTH))
        att = _dilated_attention(proj[4:7], proj[7:10], proj[10:13], attn_norm_g[l].reshape(1, ATTN_WIDTH))
        x = _ffn(x, mod, norm2_g[l].reshape(1, d), ffn2_w_in_b, ffn2_w_out_b, gf,
                 sub=2, final_norm=last, mixer=(ret, att, w_out_mix_b))
    return x
```

```python
import functools
import math

import jax
import jax.numpy as jnp
from jax import lax
from jax.experimental import pallas as pl
from jax.experimental.pallas import tpu as pltpu

RET_WIDTH = 512
RET_HEAD_DIM = 128
RET_HEADS = 4
ATTN_WIDTH = 512
ATTN_HEAD_DIM = 64
DILATED_PATTERNS = ((128, 1), (512, 4), (2048, 16))
ATTN_BLOCK = 128
ROPE_THETA = 10000.0
D_FF = 2816
N_MOD = 9
IN_COLS = 4 * RET_WIDTH + 3 * ATTN_WIDTH
EPS = 1e-6
LOG2_E = math.log2(math.e)

LANES = 128
VMEM_LIMIT = 56 * 1024 * 1024

FFN_TM = 1024
FFN_FC = 256
PROJ_TM = 1024
PROJ_TAB = 512
RET_TC = 2048
RET_STEP = 256
ATTN_TT = ATTN_BLOCK * DILATED_PATTERNS[-1][1]
BF16 = jnp.bfloat16
F32 = jnp.float32


def _sigmoid(x):
    return 1.0 / (1.0 + jnp.exp(-x))


def _norm_mod(x, g, shift, scale):
    ms = jnp.mean(x * x, axis=-1, keepdims=True)
    return (x * lax.rsqrt(ms + EPS)) * (g * (1.0 + scale)) + shift


def _mod_kernel(c_ref, w_ref, b_ref, o_ref):
    c = c_ref[...]
    ca = c * _sigmoid(c)
    o_ref[...] = jnp.dot(ca, w_ref[...], preferred_element_type=F32) + b_ref[...]


def _adaln_mod(c, w, b):
    bsz, d = c.shape
    n = w.shape[1]
    tn = 1024
    return pl.pallas_call(
        _mod_kernel,
        out_shape=jax.ShapeDtypeStruct((bsz, n), F32),
        grid=(n // tn,),
        in_specs=[
            pl.BlockSpec((bsz, d), lambda j: (0, 0)),
            pl.BlockSpec((d, tn), lambda j: (0, j)),
            pl.BlockSpec((1, tn), lambda j: (0, j)),
        ],
        out_specs=pl.BlockSpec((bsz, tn), lambda j: (0, j)),
        compiler_params=pltpu.CompilerParams(dimension_semantics=("arbitrary",)),
        name="adaln_mod",
    )(c, w, b.reshape(1, n))


def _ffn_kernel(*refs, sub, mixer_out, final_norm, n_cast):
    if mixer_out:
        x_ref, ret_ref, att_ref, wmix_ref, *refs = refs
    else:
        x_ref, *refs = refs
    mod_ref, g_ref, win_ref, wout_ref, gf_ref, *refs = refs
    cast_in, (o_ref, *refs) = refs[:n_cast], refs[n_cast:]
    cast_out, (h_ref, act_ref) = refs[:n_cast], refs[n_cast:]
    for src, dst in zip(cast_in, cast_out):
        dst[...] = src[...].astype(BF16)
    x = x_ref[...]
    if mixer_out:
        y = jnp.dot(ret_ref[...], wmix_ref[0:RET_WIDTH, :], preferred_element_type=F32)
        y = y + jnp.dot(att_ref[...], wmix_ref[RET_WIDTH:, :], preferred_element_type=F32)
        x = x + mod_ref[5:6, :] * y
    shift = mod_ref[3 * sub:3 * sub + 1, :]
    scale = mod_ref[3 * sub + 1:3 * sub + 2, :]
    gate = mod_ref[3 * sub + 2:3 * sub + 3, :]
    h_ref[...] = _norm_mod(x, g_ref[...], shift, scale).astype(BF16)
    for j in range(D_FF // FFN_FC):
        h = h_ref[...]
        a = jnp.dot(h, win_ref[:, j * FFN_FC:(j + 1) * FFN_FC], preferred_element_type=F32)
        b = jnp.dot(h, win_ref[:, D_FF + j * FFN_FC:D_FF + (j + 1) * FFN_FC], preferred_element_type=F32)
        act_ref[:, j * FFN_FC:(j + 1) * FFN_FC] = (a * _sigmoid(a) * b).astype(BF16)
    y = jnp.dot(act_ref[...], wout_ref[...], preferred_element_type=F32)
    xn = x + 0.5 * gate * y
    if final_norm:
        ms = jnp.mean(xn * xn, axis=-1, keepdims=True)
        xn = xn * lax.rsqrt(ms + EPS) * gf_ref[...]
    o_ref[...] = xn


def _ffn(x, mod, g, w_in, w_out, g_final, *, sub, final_norm, mixer=None, cast=()):
    bsz, t, d = x.shape
    tm = FFN_TM
    n_i = t // tm
    kern = functools.partial(_ffn_kernel, sub=sub, mixer_out=mixer is not None, final_norm=final_norm,
                             n_cast=len(cast))
    tok = lambda b, i: (b, i, 0)
    cast_specs = []
    for a in cast:
        chunks = next(c for c in (bsz * n_i // f for f in range(1, bsz * n_i + 1) if (bsz * n_i) % f == 0)
                      if a.shape[0] % c == 0 and (a.shape[0] // c) % 16 == 0)
        group = bsz * n_i // chunks
        cast_specs.append(pl.BlockSpec((a.shape[0] // chunks, a.shape[1]),
                                       lambda b, i, group=group: ((b * n_i + i) // group, 0)))
    mixer_specs, mixer_args = [], []
    if mixer is not None:
        ret, att, w_mix = mixer
        mixer_specs = [pl.BlockSpec((None, tm, ret.shape[-1]), tok), pl.BlockSpec((None, tm, att.shape[-1]), tok),
                       pl.BlockSpec(w_mix.shape, lambda b, i: (0, 0), pipeline_mode=pl.Buffered(1))]
        mixer_args = [ret, att, w_mix]
    out, *casted = pl.pallas_call(
        kern,
        out_shape=[jax.ShapeDtypeStruct((bsz, t, d), F32)] + [jax.ShapeDtypeStruct(a.shape, BF16) for a in cast],
        grid=(bsz, n_i),
        in_specs=[pl.BlockSpec((None, tm, d), tok)] + mixer_specs + [
            pl.BlockSpec((None, N_MOD, d), lambda b, i: (b, 0, 0)),
            pl.BlockSpec((1, d), lambda b, i: (0, 0)),
            pl.BlockSpec((d, 2 * D_FF), lambda b, i: (0, 0), pipeline_mode=pl.Buffered(1)),
            pl.BlockSpec((D_FF, d), lambda b, i: (0, 0), pipeline_mode=pl.Buffered(1)),
            pl.BlockSpec((1, d), lambda b, i: (0, 0)),
        ] + cast_specs,
        out_specs=[pl.BlockSpec((None, tm, d), tok)] + cast_specs,
        scratch_shapes=[pltpu.VMEM((tm, d), BF16), pltpu.VMEM((tm, D_FF), BF16)],
        compiler_params=pltpu.CompilerParams(
            dimension_semantics=("arbitrary", "arbitrary"), vmem_limit_bytes=VMEM_LIMIT),
        name=f"ffn{sub // 2 + 1}",
    )(x, *mixer_args, mod, g, w_in, w_out, g_final, *cast)
    return (out, casted) if cast else out


def _store_residue_layouts(val, which, p4_ref, p16_ref, t1_ref, t4_ref):
    tm, w = val.shape
    for g in range(w // LANES):
        cols = slice((3 * g + which) * LANES, (3 * g + which + 1) * LANES)
        t1_ref[g] = val[:, g * LANES:(g + 1) * LANES]
        for s4 in range(4):
            y = t1_ref[g, pl.ds(s4, tm // 4, stride=4), :]
            p4_ref[s4, :, cols] = y.astype(BF16)
            t4_ref[g, s4 * (tm // 4):(s4 + 1) * (tm // 4), :] = y
        for s16 in range(16):
            s4, u = s16 % 4, s16 // 4
            y = t4_ref[g, pl.ds(s4 * (tm // 4) + u, tm // 16, stride=4), :]
            p16_ref[s16, :, cols] = y.astype(BF16)


def _proj_kernel(x_ref, mod_ref, g_ref, w_ref, tr_ref, br_ref, ta_ref, ba_ref,
                 ret_ref, p4_ref, p16_ref, h_ref, t1_ref, t4_ref):
    x = x_ref[...]
    h_ref[...] = _norm_mod(x, g_ref[...], mod_ref[3:4, :], mod_ref[4:5, :]).astype(BF16)

    def proj(c):
        return jnp.dot(h_ref[...], w_ref[:, c * 512:(c + 1) * 512], preferred_element_type=F32)

    n_sub = x.shape[0] // PROJ_TAB
    rows_cat = lambda f: jnp.concatenate([f(i) for i in range(n_sub)], axis=0)

    def rope_ret(p, t_cos, t_sin):
        cos = rows_cat(lambda i: t_cos * br_ref[i, 0:1, :] - t_sin * br_ref[i, 1:2, :])
        sin = rows_cat(lambda i: t_cos * br_ref[i, 3:4, :] + t_sin * br_ref[i, 2:3, :])
        outs = []
        for hh in range(RET_HEADS):
            cols = slice(hh * LANES, (hh + 1) * LANES)
            xh = p[:, cols]
            outs.append(xh * cos[:, cols] + pltpu.roll(xh, RET_HEAD_DIM // 2, 1) * sin[:, cols])
        return jnp.concatenate(outs, axis=1)

    def rope_attn(p):
        t_cos, t_sin = ta_ref[0], ta_ref[1]
        cos = rows_cat(lambda i: t_cos * ba_ref[i, 0:1, :] - t_sin * ba_ref[i, 1:2, :])
        sin_lo = rows_cat(lambda i: t_cos * ba_ref[i, 3:4, :] + t_sin * ba_ref[i, 2:3, :])
        sin_hi = rows_cat(lambda i: t_cos * ba_ref[i, 5:6, :] + t_sin * ba_ref[i, 4:5, :])
        outs = []
        for hh in range(ATTN_WIDTH // LANES):
            xh = p[:, hh * LANES:(hh + 1) * LANES]
            outs.append(xh * cos + pltpu.roll(xh, 96, 1) * sin_lo + pltpu.roll(xh, 32, 1) * sin_hi)
        return jnp.concatenate(outs, axis=1)

    rw = RET_WIDTH
    ret_ref[:, 0 * rw:1 * rw] = rope_ret(proj(0), tr_ref[0], tr_ref[1]).astype(BF16)
    ret_ref[:, 1 * rw:2 * rw] = rope_ret(proj(1), tr_ref[2], tr_ref[3]).astype(BF16)
    ret_ref[:, 2 * rw:3 * rw] = proj(2).astype(BF16)
    ret_ref[:, 3 * rw:4 * rw] = proj(3).astype(BF16)
    layouts = (p4_ref, p16_ref, t1_ref, t4_ref)
    _store_residue_layouts(rope_attn(proj(4)) * (LOG2_E * ATTN_HEAD_DIM ** -0.5), 0, *layouts)
    _store_residue_layouts(rope_attn(proj(5)), 1, *layouts)
    _store_residue_layouts(proj(6), 2, *layouts)


def _rope_tables(t, tm):
    assert tm % RET_STEP == 0
    p = jnp.arange(tm, dtype=F32)[:, None]
    base = (jnp.arange(t // tm, dtype=F32) * tm)[:, None]
    ones = jnp.ones((RET_HEAD_DIM // 2,), F32)
    inv_r = ROPE_THETA ** (-jnp.arange(0, RET_HEAD_DIM, 2, dtype=F32) / RET_HEAD_DIM)
    f_r = jnp.concatenate([inv_r, inv_r])
    cos_p, sin_p = jnp.cos(p * f_r), jnp.sin(p * f_r)
    loc = p % RET_STEP
    log_g = [math.log1p(-(2.0 ** (-5.0 - hh))) for hh in range(RET_HEADS)]
    dq = [jnp.exp(loc * lg) for lg in log_g]
    dk = [jnp.exp(-loc * lg) * RET_HEAD_DIM ** -0.5 for lg in log_g]
    per_head = lambda scales, tab: jnp.concatenate([s * tab for s in scales], axis=1)
    tab_r = jnp.stack([per_head(dq, cos_p), per_head(dq, sin_p), per_head(dk, cos_p), per_head(dk, sin_p)])
    sign = jnp.concatenate([-ones, ones])
    cos_b, sin_b = jnp.cos(base * f_r), jnp.sin(base * f_r)
    heads = lambda a: jnp.concatenate([a] * RET_HEADS, axis=1)
    base_r = jnp.stack([heads(cos_b), heads(sin_b), heads(sign * cos_b), heads(sign * sin_b)], axis=1)
    inv_a = ROPE_THETA ** (-jnp.arange(0, ATTN_HEAD_DIM, 2, dtype=F32) / ATTN_HEAD_DIM)
    f_a = jnp.concatenate([inv_a] * 4)
    tab_a = jnp.stack([jnp.cos(p * f_a), jnp.sin(p * f_a)])
    half = jnp.ones((ATTN_HEAD_DIM // 2,), F32)
    pat_lo = jnp.concatenate([-half, 0 * half, -half, 0 * half])
    pat_hi = jnp.concatenate([0 * half, half, 0 * half, half])
    cos_ba, sin_ba = jnp.cos(base * f_a), jnp.sin(base * f_a)
    zero = jnp.zeros_like(cos_ba)
    base_a = jnp.stack([cos_ba, sin_ba, pat_lo * cos_ba, pat_lo * sin_ba, pat_hi * cos_ba, pat_hi * sin_ba,
                        zero, zero], axis=1)
    return tab_r, base_r, tab_a, base_a


def _mixer_in_proj(x, mod, g, w):
    bsz, t, d = x.shape
    tm = PROJ_TM
    n_sub = tm // PROJ_TAB
    tab_r, base_r, tab_a, base_a = _rope_tables(t, PROJ_TAB)
    tok = lambda b, i: (b, i, 0)
    const3 = lambda b, i: (0, 0, 0)
    aw = ATTN_WIDTH
    rw4, aw3 = 4 * RET_WIDTH, 3 * aw
    ret = jax.ShapeDtypeStruct((bsz, t, rw4), BF16)
    p4 =jax.ShapeDtypeStruct((bsz, 4, t // 4, aw3), BF16)
    p16 = jax.ShapeDtypeStruct((bsz, 16, t // 16, aw3), BF16)
    ret_spec = pl.BlockSpec((None, tm, rw4), tok)
    p4_spec = pl.BlockSpec((None, 4, tm // 4, aw3), lambda b, i: (b, 0, i, 0))
    p16_spec = pl.BlockSpec((None, 16, tm // 16, aw3), lambda b, i: (b, 0, i, 0))
    return pl.pallas_call(
        _proj_kernel,
        out_shape=[ret, p4, p16],
        grid=(bsz, t // tm),
        in_specs=[
            pl.BlockSpec((None, tm, d), tok),
            pl.BlockSpec((None, N_MOD, d), lambda b, i: (b, 0, 0)),
            pl.BlockSpec((1, d), lambda b, i: (0, 0)),
            pl.BlockSpec((d, IN_COLS), lambda b, i: (0, 0), pipeline_mode=pl.Buffered(1)),
            pl.BlockSpec(tab_r.shape, const3, pipeline_mode=pl.Buffered(1)),
            pl.BlockSpec((n_sub,) + base_r.shape[1:], lambda b, i: (i, 0, 0)),
            pl.BlockSpec(tab_a.shape, const3, pipeline_mode=pl.Buffered(1)),
            pl.BlockSpec((n_sub,) + base_a.shape[1:], lambda b, i: (i, 0, 0)),
        ],
        out_specs=[ret_spec, p4_spec, p16_spec],
        scratch_shapes=[pltpu.VMEM((tm, d), BF16),
                        pltpu.VMEM((aw // LANES, tm, LANES), F32), pltpu.VMEM((aw // LANES, tm, LANES), F32)],
        compiler_params=pltpu.CompilerParams(
            dimension_semantics=("arbitrary", "arbitrary"), vmem_limit_bytes=VMEM_LIMIT),
        name="mixer_in_proj",
    )(x, mod, g, w, tab_r, base_r, tab_a, base_a)


def _block_diag(a, b):
    za, zb = jnp.zeros_like(a), jnp.zeros_like(b)
    return jnp.concatenate([jnp.concatenate([a, zb], axis=1), jnp.concatenate([za, b], axis=1)], axis=0)


def _ret_kernel(q_ref, k_ref, v_ref, g_ref, gn_ref, o_ref, state_ref, tri_ref):
    c, d = RET_STEP, RET_HEAD_DIM
    log_g = [math.log1p(-(2.0 ** (-5.0 - hh))) for hh in range(RET_HEADS)]

    @pl.when(jnp.logical_and(pl.program_id(0) == 0, pl.program_id(1) == 0))
    def _():
        ii = lax.broadcasted_iota(jnp.int32, (c, c), 0)
        jj = lax.broadcasted_iota(jnp.int32, (c, c), 1)
        tri_ref[...] = jnp.where(ii >= jj, 1.0, 0.0)

    @pl.when(pl.program_id(1) == 0)
    def _():
        state_ref[...] = jnp.zeros_like(state_ref)

    pairs = RET_HEADS // 2
    steps = RET_TC // c
    contract_rows = (((0,), (0,)), ((), ()))
    kv = {}
    for hp in range(pairs):
        cols = slice(2 * hp * d, 2 * (hp + 1) * d)
        for n in range(steps):
            rows = slice(n * c, (n + 1) * c)
            kv2 = lax.dot_general(k_ref[rows, cols], v_ref[rows, cols], contract_rows,
                                  preferred_element_type=F32)
            kv[hp, n] = (kv2[:d, :d], kv2[d:, d:])
    seen = {}
    for hp in range(pairs):
        for hh in range(2):
            h = 2 * hp + hh
            st = state_ref[h]
            for n in range(steps):
                seen[h, n] = (math.exp(log_g[h]) * st).astype(BF16)
                st = math.exp(c * log_g[h]) * st + math.exp((c - 1) * log_g[h]) * kv[hp, n][hh]
            state_ref[h] = st
    tri2 = jnp.concatenate([tri_ref[...], tri_ref[...]], axis=1)
    for hp in range(pairs):
        cols = slice(2 * hp * d, 2 * (hp + 1) * d)
        for n in range(steps):
            rows = slice(n * c, (n + 1) * c)
            q2, k2, v2 = q_ref[rows, cols], k_ref[rows, cols], v_ref[rows, cols]
            kbd = _block_diag(k2[:, :d], k2[:, d:])
            s2 = lax.dot_general(q2, kbd, (((1,), (1,)), ((), ())), preferred_element_type=F32) * tri2
            inner = jnp.dot(s2.astype(BF16), _block_diag(v2[:, :d], v2[:, d:]), preferred_element_type=F32)
            cross = jnp.dot(q2, _block_diag(seen[2 * hp, n], seen[2 * hp + 1, n]), preferred_element_type=F32)
            o2 = inner + cross
            for hh in range(2):
                hc = slice((2 * hp + hh) * d, (2 * hp + hh + 1) * d)
                o = o2[:, hh * d:(hh + 1) * d]
                mu = jnp.mean(o, axis=-1, keepdims=True)
                oc = o - mu
                var = jnp.mean(oc * oc, axis=-1, keepdims=True)
                gt = g_ref[rows, hc].astype(F32)
                o_ref[rows, hc] = (oc * lax.rsqrt(var + EPS) * gn_ref[:, hc] * (gt * _sigmoid(gt))).astype(BF16)


def _retention(qkvg, gn):
    bsz, t, _ = qkvg.shape
    w = RET_WIDTH
    group = lambda k: pl.BlockSpec((None, RET_TC, w), lambda b, i, k=k: (b, i, k))
    return pl.pallas_call(
        _ret_kernel,
        out_shape=jax.ShapeDtypeStruct((bsz, t, w), BF16),
        grid=(bsz, t // RET_TC),
        in_specs=[group(0), group(1), group(2), group(3), pl.BlockSpec((1, w), lambda b, i: (0, 0))],
        out_specs=group(0),
        scratch_shapes=[
            pltpu.VMEM((RET_HEADS, RET_HEAD_DIM, RET_HEAD_DIM), F32),
            pltpu.VMEM((RET_STEP, RET_STEP), F32),
        ],
        compiler_params=pltpu.CompilerParams(
            dimension_semantics=("arbitrary", "arbitrary"), vmem_limit_bytes=VMEM_LIMIT),
        name="retention",
    )(qkvg, qkvg, qkvg, qkvg, gn)


def _attn_unit(q2, k2, v2, bias, lo):
    blk = ATTN_BLOCK
    zero = jnp.zeros_like(q2)
    qs = jnp.concatenate([jnp.where(lo, q2, zero), jnp.where(lo, zero, q2)], axis=0)
    sc = lax.dot_general(qs, k2, (((1,), (1,)), ((), ())), preferred_element_type=F32)
    sc = sc + jnp.concatenate([bias, bias], axis=0)
    m = jnp.max(sc, axis=-1, keepdims=True)
    p = jnp.exp2(sc - m).astype(BF16)
    va = jnp.concatenate([v2, jnp.ones_like(v2)], axis=1)
    oa = jnp.dot(p, va, preferred_element_type=F32)
    o = jnp.where(lo, oa[:blk, :LANES], oa[blk:, :LANES])
    l = jnp.where(lo, oa[:blk, LANES:], oa[blk:, LANES:])
    mm = jnp.where(lo, jnp.broadcast_to(m[:blk], (blk, LANES)), jnp.broadcast_to(m[blk:], (blk, LANES)))
    return o * (1.0 / l), mm + jnp.log2(l)


def _attn_kernel(c4_ref, p4_ref, c16_ref, p16_ref, g_ref, o_ref, bias_ref, ob_ref, lb_ref, tb_ref):
    tt, blk = ATTN_TT, ATTN_BLOCK
    tile = pl.program_id(2)
    lane = lax.broadcasted_iota(jnp.int32, (blk, LANES), 1)
    lo = lane < ATTN_HEAD_DIM
    qi = lax.broadcasted_iota(jnp.int32, (blk, 2 * blk), 0)
    kj = lax.broadcasted_iota(jnp.int32, (blk, 2 * blk), 1)
    neg = jnp.float32(-jnp.inf)

    def band_tables(slot, qi, kj):
        band = jnp.where((kj >= qi) & (kj <= qi + blk), 0.0, neg)
        bias_ref[slot] = band
        bias_ref[slot + 1] = jnp.where(jnp.logical_or(kj >= blk, tile > 0), band, neg)

    band_tables(0, qi, kj)
    sub = blk // 4
    band_tables(2, 4 * (qi % sub) + qi // sub, 4 * (kj % (2 * sub)) + kj // (2 * sub))

    cur_refs = (c4_ref, c4_ref, c16_ref)
    prev_refs = (p4_ref, p4_ref, p16_ref)

    def rows_of(ref, which, r, s, lo_row, n):
        lanes = slice(which * LANES, (which + 1) * LANES)
        if r == 1:
            return jnp.concatenate([ref[c, lo_row // 4:(lo_row + n) // 4, lanes] for c in range(4)], axis=0)
        return ref[s, lo_row:lo_row + n, lanes]

    def keys(bi, which, r, s, j):
        if j == 0 and r == 1:
            return jnp.concatenate([piece for c in range(4) for piece in (
                prev_refs[bi][c, blk - sub:blk, which * LANES:(which + 1) * LANES],
                cur_refs[bi][c, 0:sub, which * LANES:(which + 1) * LANES])], axis=0)
        if j == 0:
            return jnp.concatenate([rows_of(prev_refs[bi], which, r, s, 0, blk),
                                    rows_of(cur_refs[bi], which, r, s, 0, blk)], axis=0)
        return rows_of(cur_refs[bi], which, r, s, (j - 1) * blk, 2 * blk)

    for bi, (window, r) in enumerate(DILATED_PATTERNS):
        assert window // r == blk
        for s in range(r):
            for j in range(tt // (blk * r)):
                q2 = rows_of(cur_refs[bi], 0, r, s, j * blk, blk)
                k2 = keys(bi, 1, r, s, j)
                v2 = keys(bi, 2, r, s, j)
                o, lse = _attn_unit(q2, k2, v2, bias_ref[(2 if r == 1 else 0) + (1 if j == 0 else 0)], lo)
                if r == 1:
                    for c in range(4):
                        rows = pl.ds(j * blk + c, sub, stride=4)
                        ob_ref[bi, rows, :] = o[c * sub:(c + 1) * sub]
                        lb_ref[bi, rows, :] = lse[c * sub:(c + 1) * sub]
                elif r == 16:
                    rows = pl.ds((s % 4) * (tt // 4) + s // 4, blk, stride=4)
                    tb_ref[0, rows, :] = o
                    tb_ref[1, rows, :] = lse
                else:
                    rows = pl.ds(s + r * blk * j, blk, stride=r)
                    ob_ref[bi, rows, :] = o
                    lb_ref[bi, rows, :] = lse
        if r == 16:
            for s4 in range(4):
                src = slice(s4 * (tt // 4), (s4 + 1) * (tt // 4))
                ob_ref[bi, pl.ds(s4, tt // 4, stride=4), :] = tb_ref[0, src, :]
                lb_ref[bi, pl.ds(s4, tt // 4, stride=4), :] = tb_ref[1, src, :]

    rc = 256
    li = lax.broadcasted_iota(jnp.int32, (LANES, LANES), 0) // ATTN_HEAD_DIM
    lj = lax.broadcasted_iota(jnp.int32, (LANES, LANES), 1) // ATTN_HEAD_DIM
    head_mean = jnp.where(li == lj, 1.0 / ATTN_HEAD_DIM, 0.0).astype(BF16)
    gain = g_ref[...]

    for ci in range(tt // rc):
        rows = slice(ci * rc, (ci + 1) * rc)
        l0, l1, l2 = lb_ref[0, rows, :], lb_ref[1, rows, :], lb_ref[2, rows, :]
        mx = jnp.maximum(jnp.maximum(l0, l1), l2)
        e0, e1, e2 = jnp.exp2(l0 - mx), jnp.exp2(l1 - mx), jnp.exp2(l2 - mx)
        att = (e0 * ob_ref[0, rows, :] + e1 * ob_ref[1, rows, :] + e2 * ob_ref[2, rows, :]) * (1.0 / (e0 + e1 + e2))
        ms = jnp.dot((att * att).astype(BF16), head_mean, preferred_element_type=F32)
        o_ref[rows, :] = (att * lax.rsqrt(ms + EPS) * gain).astype(BF16)


def _dilated_attention(layouts, gain):
    assert [r for _, r in DILATED_PATTERNS] == [1, 4, 16]
    bsz, r4, t4, w3 = layouts[0].shape
    t, w = r4 * t4, w3 // 3
    tt, blk = ATTN_TT, ATTN_BLOCK
    ngrp = w // LANES
    gw = 3 * LANES

    def cur_prev(r):
        rows = tt // r
        cur = pl.BlockSpec((None, r, rows, gw), lambda b, hp, i: (b, 0, i, hp))
        prev = pl.BlockSpec((None, r, blk, gw),
                            lambda b, hp, i: (b, 0, jnp.maximum(i * (rows // blk) - 1, 0), hp))
        return [cur, prev]

    return pl.pallas_call(
        _attn_kernel,
        out_shape=jax.ShapeDtypeStruct((bsz, t, w), BF16),
        grid=(bsz, ngrp, t // tt),
        in_specs=[sp for _, r in DILATED_PATTERNS[1:] for sp in cur_prev(r)]
        + [pl.BlockSpec((1, LANES), lambda b, hp, i: (0, hp))],
        out_specs=pl.BlockSpec((None, tt, LANES), lambda b, hp, i: (b, i, hp)),
        scratch_shapes=[
            pltpu.VMEM((4, blk, 2 * blk), F32),
            pltpu.VMEM((3, tt, LANES), F32), pltpu.VMEM((3, tt, LANES), F32), pltpu.VMEM((2, tt, LANES), F32),
        ],
        compiler_params=pltpu.CompilerParams(
            dimension_semantics=("arbitrary", "arbitrary", "arbitrary"), vmem_limit_bytes=VMEM_LIMIT),
        name="dilated_attention",
    )(*[a for a in layouts for _ in range(2)], gain)


def kernel(x, c, w_ada, b_ada, norm1_g, ffn1_w_in, ffn1_w_out, norm_mix_g, w_in_mix, ret_gn_g, attn_norm_g,
           w_out_mix, norm2_g, ffn2_w_in, ffn2_w_out, norm_f_g):
    depth = w_ada.shape[0]
    assert depth >= 1, "the final RMSNorm is fused into the last layer's second FFN"
    bsz, t, d = x.shape
    gf = norm_f_g.reshape(1, d)
    for l in range(depth):
        last = l == depth - 1
        mod = _adaln_mod(c, w_ada[l], b_ada[l]).reshape(bsz, N_MOD, d)
        x, (w_in_mix_b, w_out_mix_b, ffn2_w_in_b, ffn2_w_out_b) = _ffn(
            x, mod, norm1_g[l].reshape(1, d), ffn1_w_in[l].astype(BF16), ffn1_w_out[l].astype(BF16), gf,
            sub=0, final_norm=False, cast=(w_in_mix[l], w_out_mix[l], ffn2_w_in[l], ffn2_w_out[l]))
        ret_qkvg, *att_layouts = _mixer_in_proj(x, mod, norm_mix_g[l].reshape(1, d), w_in_mix_b)
        ret = _retention(ret_qkvg, ret_gn_g[l].reshape(1, RET_WIDTH))
        att = _dilated_attention(att_layouts, attn_norm_g[l].reshape(1, ATTN_WIDTH))
        x = _ffn(x, mod, norm2_g[l].reshape(1, d), ffn2_w_in_b, ffn2_w_out_b, gf,
                 sub=2, final_norm=last, mixer=(ret, att, w_out_mix_b))
    return x
```

```python
import functools
import math

import jax
import jax.numpy as jnp
from jax import lax
from jax.experimental import pallas as pl
from jax.experimental.pallas import tpu as pltpu

RET_WIDTH = 512
RET_HEAD_DIM = 128
RET_HEADS = 4
ATTN_WIDTH = 512
ATTN_HEAD_DIM = 64
DILATED_PATTERNS = ((128, 1), (512, 4), (2048, 16))
ATTN_BLOCK = 128
ROPE_THETA = 10000.0
D_FF = 2816
N_MOD = 9
IN_COLS = 4 * RET_WIDTH + 3 * ATTN_WIDTH
EPS = 1e-6
LOG2_E = math.log2(math.e)

LANES = 128
VMEM_LIMIT = 56 * 1024 * 1024

FFN_TM = 1024
FFN_FC = 256
PROJ_TM = 1024
PROJ_TAB = 512
RET_TC = 2048
RET_STEP = 256
ATTN_TT = ATTN_BLOCK * DILATED_PATTERNS[-1][1]
BF16 = jnp.bfloat16
F32 = jnp.float32


def _sigmoid(x):
    return 1.0 / (1.0 + jnp.exp(-x))


def _norm_mod(x, g, shift, scale):
    ms = jnp.mean(x * x, axis=-1, keepdims=True)
    return (x * lax.rsqrt(ms + EPS)) * (g * (1.0 + scale)) + shift


def _mod_kernel(c_ref, w_ref, b_ref, o_ref):
    c = c_ref[...]
    ca = c * _sigmoid(c)
    o_ref[...] = jnp.dot(ca, w_ref[...], preferred_element_type=F32) + b_ref[...]


def _adaln_mod(c, w, b):
    bsz, d = c.shape
    n = w.shape[1]
    tn = 1024
    return pl.pallas_call(
        _mod_kernel,
        out_shape=jax.ShapeDtypeStruct((bsz, n), F32),
        grid=(n // tn,),
        in_specs=[
            pl.BlockSpec((bsz, d), lambda j: (0, 0)),
            pl.BlockSpec((d, tn), lambda j: (0, j)),
            pl.BlockSpec((1, tn), lambda j: (0, j)),
        ],
        out_specs=pl.BlockSpec((bsz, tn), lambda j: (0, j)),
        compiler_params=pltpu.CompilerParams(dimension_semantics=("arbitrary",)),
        name="adaln_mod",
    )(c, w, b.reshape(1, n))


def _ffn_kernel(*refs, sub, mixer_out, final_norm, n_cast):
    if mixer_out:
        x_ref, ret_ref, att_ref, wmix_ref, *refs = refs
    else:
        x_ref, *refs = refs
    mod_ref, g_ref, win_ref, wout_ref, gf_ref, *refs = refs
    cast_in, (o_ref, *refs) = refs[:n_cast], refs[n_cast:]
    cast_out, (h_ref, act_ref) = refs[:n_cast], refs[n_cast:]
    for src, dst in zip(cast_in, cast_out):
        dst[...] = src[...].astype(BF16)
    x = x_ref[...]
    if mixer_out:
        y = jnp.dot(jnp.concatenate([ret_ref[...], att_ref[...]], axis=1), wmix_ref[...],
                    preferred_element_type=F32)
        x = x + mod_ref[5:6, :] * y
    shift = mod_ref[3 * sub:3 * sub + 1, :]
    scale = mod_ref[3 * sub + 1:3 * sub + 2, :]
    gate = mod_ref[3 * sub + 2:3 * sub + 3, :]
    h_ref[...] = _norm_mod(x, g_ref[...], shift, scale).astype(BF16)
    for j in range(D_FF // FFN_FC):
        h = h_ref[...]
        a = jnp.dot(h, win_ref[:, j * FFN_FC:(j + 1) * FFN_FC], preferred_element_type=F32)
        b = jnp.dot(h, win_ref[:, D_FF + j * FFN_FC:D_FF + (j + 1) * FFN_FC], preferred_element_type=F32)
        act_ref[:, j * FFN_FC:(j + 1) * FFN_FC] = (a * _sigmoid(a) * b).astype(BF16)
    y = jnp.dot(act_ref[...], wout_ref[...], preferred_element_type=F32)
    xn = x + 0.5 * gate * y
    if final_norm:
        ms = jnp.mean(xn * xn, axis=-1, keepdims=True)
        xn = xn * lax.rsqrt(ms + EPS) * gf_ref[...]
    o_ref[...] = xn


def _ffn(x, mod, g, w_in, w_out, g_final, *, sub, final_norm, mixer=None, cast=()):
    bsz, t, d = x.shape
    tm = FFN_TM
    n_i = t // tm
    kern = functools.partial(_ffn_kernel, sub=sub, mixer_out=mixer is not None, final_norm=final_norm,
                             n_cast=len(cast))
    tok = lambda b, i: (b, i, 0)
    cast_specs = []
    for a in cast:
        chunks = next(c for c in (bsz * n_i // f for f in range(1, bsz * n_i + 1) if (bsz * n_i) % f == 0)
                      if a.shape[0] % c == 0 and (a.shape[0] // c) % 16 == 0)
        group = bsz * n_i // chunks
        cast_specs.append(pl.BlockSpec((a.shape[0] // chunks, a.shape[1]),
                                       lambda b, i, group=group: ((b * n_i + i) // group, 0)))
    mixer_specs, mixer_args = [], []
    if mixer is not None:
        ret, att, w_mix = mixer
        mixer_specs = [pl.BlockSpec((None, tm, ret.shape[-1]), tok), pl.BlockSpec((None, tm, att.shape[-1]), tok),
                       pl.BlockSpec(w_mix.shape, lambda b, i: (0, 0), pipeline_mode=pl.Buffered(1))]
        mixer_args = [ret, att, w_mix]
    out, *casted = pl.pallas_call(
        kern,
        out_shape=[jax.ShapeDtypeStruct((bsz, t, d), F32)] + [jax.ShapeDtypeStruct(a.shape, BF16) for a in cast],
        grid=(bsz, n_i),
        in_specs=[pl.BlockSpec((None, tm, d), tok)] + mixer_specs + [
            pl.BlockSpec((None, N_MOD, d), lambda b, i: (b, 0, 0)),
            pl.BlockSpec((1, d), lambda b, i: (0, 0)),
            pl.BlockSpec((d, 2 * D_FF), lambda b, i: (0, 0), pipeline_mode=pl.Buffered(1)),
            pl.BlockSpec((D_FF, d), lambda b, i: (0, 0), pipeline_mode=pl.Buffered(1)),
            pl.BlockSpec((1, d), lambda b, i: (0, 0)),
        ] + cast_specs,
        out_specs=[pl.BlockSpec((None, tm, d), tok)] + cast_specs,
        scratch_shapes=[pltpu.VMEM((tm, d), BF16), pltpu.VMEM((tm, D_FF), BF16)],
        compiler_params=pltpu.CompilerParams(
            dimension_semantics=("arbitrary", "arbitrary"), vmem_limit_bytes=VMEM_LIMIT),
        name=f"ffn{sub // 2 + 1}",
    )(x, *mixer_args, mod, g, w_in, w_out, g_final, *cast)
    return (out, casted) if cast else out


def _store_residue_layouts(val, which, p4_ref, p16_ref, t1_ref, t4_ref):
    tm, w = val.shape
    for g in range(w // LANES):
        cols = slice((3 * g + which) * LANES, (3 * g + which + 1) * LANES)
        t1_ref[g] = val[:, g * LANES:(g + 1) * LANES]
        for s4 in range(4):
            y = t1_ref[g, pl.ds(s4, tm // 4, stride=4), :]
            p4_ref[s4, :, cols] = y.astype(BF16)
            t4_ref[g, s4 * (tm // 4):(s4 + 1) * (tm // 4), :] = y
        for s16 in range(16):
            s4, u = s16 % 4, s16 // 4
            y = t4_ref[g, pl.ds(s4 * (tm // 4) + u, tm // 16, stride=4), :]
            p16_ref[s16, :, cols] = y.astype(BF16)


def _proj_kernel(x_ref, mod_ref, g_ref, w_ref, tr_ref, br_ref, ta_ref, ba_ref,
                 ret_ref, p4_ref, p16_ref, h_ref, t1_ref, t4_ref):
    x = x_ref[...]
    h_ref[...] = _norm_mod(x, g_ref[...], mod_ref[3:4, :], mod_ref[4:5, :]).astype(BF16)

    def proj(c):
        return jnp.dot(h_ref[...], w_ref[:, c * 512:(c + 1) * 512], preferred_element_type=F32)

    n_sub = x.shape[0] // PROJ_TAB
    rows_cat = lambda f: jnp.concatenate([f(i) for i in range(n_sub)], axis=0)

    def rope_ret(p, t_cos, t_sin):
        cos = rows_cat(lambda i: t_cos * br_ref[i, 0:1, :] - t_sin * br_ref[i, 1:2, :])
        sin = rows_cat(lambda i: t_cos * br_ref[i, 3:4, :] + t_sin * br_ref[i, 2:3, :])
        outs = []
        for hh in range(RET_HEADS):
            cols = slice(hh * LANES, (hh + 1) * LANES)
            xh = p[:, cols]
            outs.append(xh * cos[:, cols] + pltpu.roll(xh, RET_HEAD_DIM // 2, 1) * sin[:, cols])
        return jnp.concatenate(outs, axis=1)

    def rope_attn(p):
        t_cos, t_sin = ta_ref[0], ta_ref[1]
        cos = rows_cat(lambda i: t_cos * ba_ref[i, 0:1, :] - t_sin * ba_ref[i, 1:2, :])
        sin_lo = rows_cat(lambda i: t_cos * ba_ref[i, 3:4, :] + t_sin * ba_ref[i, 2:3, :])
        sin_hi = rows_cat(lambda i: t_cos * ba_ref[i, 5:6, :] + t_sin * ba_ref[i, 4:5, :])
        outs = []
        for hh in range(ATTN_WIDTH // LANES):
            xh = p[:, hh * LANES:(hh + 1) * LANES]
            outs.append(xh * cos + pltpu.roll(xh, 96, 1) * sin_lo + pltpu.roll(xh, 32, 1) * sin_hi)
        return jnp.concatenate(outs, axis=1)

    rw = RET_WIDTH
    ret_ref[:, 0 * rw:1 * rw] = rope_ret(proj(0), tr_ref[0], tr_ref[1]).astype(BF16)
    ret_ref[:, 1 * rw:2 * rw] = rope_ret(proj(1), tr_ref[2], tr_ref[3]).astype(BF16)
    ret_ref[:, 2 * rw:3 * rw] = proj(2).astype(BF16)
    ret_ref[:, 3 * rw:4 * rw] = proj(3).astype(BF16)
    layouts = (p4_ref, p16_ref, t1_ref, t4_ref)
    _store_residue_layouts(rope_attn(proj(4)) * (LOG2_E * ATTN_HEAD_DIM ** -0.5), 0, *layouts)
    _store_residue_layouts(rope_attn(proj(5)), 1, *layouts)
    _store_residue_layouts(proj(6), 2, *layouts)


def _rope_tables(t, tm):
    assert tm % RET_STEP == 0
    p = jnp.arange(tm, dtype=F32)[:, None]
    base = (jnp.arange(t // tm, dtype=F32) * tm)[:, None]
    ones = jnp.ones((RET_HEAD_DIM // 2,), F32)
    inv_r = ROPE_THETA ** (-jnp.arange(0, RET_HEAD_DIM, 2, dtype=F32) / RET_HEAD_DIM)
    f_r = jnp.concatenate([inv_r, inv_r])
    cos_p, sin_p = jnp.cos(p * f_r), jnp.sin(p * f_r)
    loc = p % RET_STEP
    log_g = [math.log1p(-(2.0 ** (-5.0 - hh))) for hh in range(RET_HEADS)]
    dq = [jnp.exp(loc * lg) for lg in log_g]
    dk = [jnp.exp(-loc * lg) * RET_HEAD_DIM ** -0.5 for lg in log_g]
    per_head = lambda scales, tab: jnp.concatenate([s * tab for s in scales], axis=1)
    tab_r = jnp.stack([per_head(dq, cos_p), per_head(dq, sin_p), per_head(dk, cos_p), per_head(dk, sin_p)])
    sign = jnp.concatenate([-ones, ones])
    cos_b, sin_b = jnp.cos(base * f_r), jnp.sin(base * f_r)
    heads = lambda a: jnp.concatenate([a] * RET_HEADS, axis=1)
    base_r = jnp.stack([heads(cos_b), heads(sin_b), heads(sign * cos_b), heads(sign * sin_b)], axis=1)
    inv_a = ROPE_THETA ** (-jnp.arange(0, ATTN_HEAD_DIM, 2, dtype=F32) / ATTN_HEAD_DIM)
    f_a = jnp.concatenate([inv_a] * 4)
    tab_a = jnp.stack([jnp.cos(p * f_a), jnp.sin(p * f_a)])
    half = jnp.ones((ATTN_HEAD_DIM // 2,), F32)
    pat_lo = jnp.concatenate([-half, 0 * half, -half, 0 * half])
    pat_hi = jnp.concatenate([0 * half, half, 0 * half, half])
    cos_ba, sin_ba = jnp.cos(base * f_a), jnp.sin(base * f_a)
    zero = jnp.zeros_like(cos_ba)
    base_a = jnp.stack([cos_ba, sin_ba, pat_lo * cos_ba, pat_lo * sin_ba, pat_hi * cos_ba, pat_hi * sin_ba,
                        zero, zero], axis=1)
    return tab_r, base_r, tab_a, base_a


def _mixer_in_proj(x, mod, g, w):
    bsz, t, d = x.shape
    tm = PROJ_TM
    n_sub = tm // PROJ_TAB
    tab_r, base_r, tab_a, base_a = _rope_tables(t, PROJ_TAB)
    tok = lambda b, i: (b, i, 0)
    const3 = lambda b, i: (0, 0, 0)
    aw = ATTN_WIDTH
    rw4, aw3 = 4 * RET_WIDTH, 3 * aw
    ret = jax.ShapeDtypeStruct((bsz, t, rw4), BF16)
    p4 =jax.ShapeDtypeStruct((bsz, 4, t // 4, aw3), BF16)
    p16 = jax.ShapeDtypeStruct((bsz, 16, t // 16, aw3), BF16)
    ret_spec = pl.BlockSpec((None, tm, rw4), tok)
    p4_spec = pl.BlockSpec((None, 4, tm // 4, aw3), lambda b, i: (b, 0, i, 0))
    p16_spec = pl.BlockSpec((None, 16, tm // 16, aw3), lambda b, i: (b, 0, i, 0))
    return pl.pallas_call(
        _proj_kernel,
        out_shape=[ret, p4, p16],
        grid=(bsz, t // tm),
        in_specs=[
            pl.BlockSpec((None, tm, d), tok),
            pl.BlockSpec((None, N_MOD, d), lambda b, i: (b, 0, 0)),
            pl.BlockSpec((1, d), lambda b, i: (0, 0)),
            pl.BlockSpec((d, IN_COLS), lambda b, i: (0, 0), pipeline_mode=pl.Buffered(1)),
            pl.BlockSpec(tab_r.shape, const3, pipeline_mode=pl.Buffered(1)),
            pl.BlockSpec((n_sub,) + base_r.shape[1:], lambda b, i: (i, 0, 0)),
            pl.BlockSpec(tab_a.shape, const3, pipeline_mode=pl.Buffered(1)),
            pl.BlockSpec((n_sub,) + base_a.shape[1:], lambda b, i: (i, 0, 0)),
        ],
        out_specs=[ret_spec, p4_spec, p16_spec],
        scratch_shapes=[pltpu.VMEM((tm, d), BF16),
                        pltpu.VMEM((aw // LANES, tm, LANES), F32), pltpu.VMEM((aw // LANES, tm, LANES), F32)],
        compiler_params=pltpu.CompilerParams(
            dimension_semantics=("arbitrary", "arbitrary"), vmem_limit_bytes=VMEM_LIMIT),
        name="mixer_in_proj",
    )(x, mod, g, w, tab_r, base_r, tab_a, base_a)


def _block_diag(a, b):
    za, zb = jnp.zeros_like(a), jnp.zeros_like(b)
    return jnp.concatenate([jnp.concatenate([a, zb], axis=1), jnp.concatenate([za, b], axis=1)], axis=0)


def _ret_kernel(q_ref, k_ref, v_ref, g_ref, gn_ref, o_ref, state_ref, tri_ref):
    c, d = RET_STEP, RET_HEAD_DIM
    log_g = [math.log1p(-(2.0 ** (-5.0 - hh))) for hh in range(RET_HEADS)]

    @pl.when(jnp.logical_and(pl.program_id(0) == 0, pl.program_id(1) == 0))
    def _():
        ii = lax.broadcasted_iota(jnp.int32, (c, c), 0)
        jj = lax.broadcasted_iota(jnp.int32, (c, c), 1)
        tri_ref[...] = jnp.where(ii >= jj, 1.0, 0.0)

    @pl.when(pl.program_id(1) == 0)
    def _():
        state_ref[...] = jnp.zeros_like(state_ref)

    pairs = RET_HEADS // 2
    steps = RET_TC // c
    contract_rows = (((0,), (0,)), ((), ()))
    kv = {}
    for hp in range(pairs):
        cols = slice(2 * hp * d, 2 * (hp + 1) * d)
        for n in range(steps):
            rows = slice(n * c, (n + 1) * c)
            kv2 = lax.dot_general(k_ref[rows, cols], v_ref[rows, cols], contract_rows,
                                  preferred_element_type=F32)
            kv[hp, n] = (kv2[:d, :d], kv2[d:, d:])
    seen = {}
    for hp in range(pairs):
        for hh in range(2):
            h = 2 * hp + hh
            st = state_ref[h]
            for n in range(steps):
                seen[h, n] = (math.exp(log_g[h]) * st).astype(BF16)
                st = math.exp(c * log_g[h]) * st + math.exp((c - 1) * log_g[h]) * kv[hp, n][hh]
            state_ref[h] = st
    tri2 = jnp.concatenate([tri_ref[...], tri_ref[...]], axis=1)
    for hp in range(pairs):
        cols = slice(2 * hp * d, 2 * (hp + 1) * d)
        for n in range(steps):
            rows = slice(n * c, (n + 1) * c)
            q2, k2, v2 = q_ref[rows, cols], k_ref[rows, cols], v_ref[rows, cols]
            kbd = _block_diag(k2[:, :d], k2[:, d:])
            s2 = lax.dot_general(q2, kbd, (((1,), (1,)), ((), ())), preferred_element_type=F32) * tri2
            inner = jnp.dot(s2.astype(BF16), _block_diag(v2[:, :d], v2[:, d:]), preferred_element_type=F32)
            cross = jnp.dot(q2, _block_diag(seen[2 * hp, n], seen[2 * hp + 1, n]), preferred_element_type=F32)
            o2 = inner + cross
            for hh in range(2):
                hc = slice((2 * hp + hh) * d, (2 * hp + hh + 1) * d)
                o = o2[:, hh * d:(hh + 1) * d]
                mu = jnp.mean(o, axis=-1, keepdims=True)
                oc = o - mu
                var = jnp.mean(oc * oc, axis=-1, keepdims=True)
                gt = g_ref[rows, hc].astype(F32)
                o_ref[rows, hc] = (oc * lax.rsqrt(var + EPS) * gn_ref[:, hc] * (gt * _sigmoid(gt))).astype(BF16)


def _retention(qkvg, gn):
    bsz, t, _ = qkvg.shape
    w = RET_WIDTH
    group = lambda k: pl.BlockSpec((None, RET_TC, w), lambda b, i, k=k: (b, i, k))
    return pl.pallas_call(
        _ret_kernel,
        out_shape=jax.ShapeDtypeStruct((bsz, t, w), BF16),
        grid=(bsz, t // RET_TC),
        in_specs=[group(0), group(1), group(2), group(3), pl.BlockSpec((1, w), lambda b, i: (0, 0))],
        out_specs=group(0),
        scratch_shapes=[
            pltpu.VMEM((RET_HEADS, RET_HEAD_DIM, RET_HEAD_DIM), F32),
            pltpu.VMEM((RET_STEP, RET_STEP), F32),
        ],
        compiler_params=pltpu.CompilerParams(
            dimension_semantics=("arbitrary", "arbitrary"), vmem_limit_bytes=VMEM_LIMIT),
        name="retention",
    )(qkvg, qkvg, qkvg, qkvg, gn)


def _attn_unit(q2, k2, v2, bias, lo):
    blk = ATTN_BLOCK
    zero = jnp.zeros_like(q2)
    qs = jnp.concatenate([jnp.where(lo, q2, zero), jnp.where(lo, zero, q2)], axis=0)
    sc = lax.dot_general(qs, k2, (((1,), (1,)), ((), ())), preferred_element_type=F32)
    sc = sc + jnp.concatenate([bias, bias], axis=0)
    m = jnp.max(sc, axis=-1, keepdims=True)
    p = jnp.exp2(sc - m).astype(BF16)
    va = jnp.concatenate([v2, jnp.ones_like(v2)], axis=1)
    oa = jnp.dot(p, va, preferred_element_type=F32)
    o = jnp.where(lo, oa[:blk, :LANES], oa[blk:, :LANES])
    l = jnp.where(lo, oa[:blk, LANES:], oa[blk:, LANES:])
    mm = jnp.where(lo, jnp.broadcast_to(m[:blk], (blk, LANES)), jnp.broadcast_to(m[blk:], (blk, LANES)))
    return o * (1.0 / l), mm + jnp.log2(l)


def _attn_kernel(c4_ref, p4_ref, c16_ref, p16_ref, g_ref, o_ref, bias_ref, ob_ref, lb_ref, tb_ref):
    tt, blk = ATTN_TT, ATTN_BLOCK
    tile = pl.program_id(2)
    lane = lax.broadcasted_iota(jnp.int32, (blk, LANES), 1)
    lo = lane < ATTN_HEAD_DIM
    qi = lax.broadcasted_iota(jnp.int32, (blk, 2 * blk), 0)
    kj = lax.broadcasted_iota(jnp.int32, (blk, 2 * blk), 1)
    neg = jnp.float32(-jnp.inf)

    def band_tables(slot, qi, kj):
        band = jnp.where((kj >= qi) & (kj <= qi + blk), 0.0, neg)
        bias_ref[slot] = band
        bias_ref[slot + 1] = jnp.where(jnp.logical_or(kj >= blk, tile > 0), band, neg)

    band_tables(0, qi, kj)
    sub = blk // 4
    band_tables(2, 4 * (qi % sub) + qi // sub, 4 * (kj % (2 * sub)) + kj // (2 * sub))

    cur_refs = (c4_ref, c4_ref, c16_ref)
    prev_refs = (p4_ref, p4_ref, p16_ref)

    def rows_of(ref, which, r, s, lo_row, n):
        lanes = slice(which * LANES, (which + 1) * LANES)
        if r == 1:
            return jnp.concatenate([ref[c, lo_row // 4:(lo_row + n) // 4, lanes] for c in range(4)], axis=0)
        return ref[s, lo_row:lo_row + n, lanes]

    def keys(bi, which, r, s, j):
        if j == 0 and r == 1:
            return jnp.concatenate([piece for c in range(4) for piece in (
                prev_refs[bi][c, blk - sub:blk, which * LANES:(which + 1) * LANES],
                cur_refs[bi][c, 0:sub, which * LANES:(which + 1) * LANES])], axis=0)
        if j == 0:
            return jnp.concatenate([rows_of(prev_refs[bi], which, r, s, 0, blk),
                                    rows_of(cur_refs[bi], which, r, s, 0, blk)], axis=0)
        return rows_of(cur_refs[bi], which, r, s, (j - 1) * blk, 2 * blk)

    for bi, (window, r) in enumerate(DILATED_PATTERNS):
        assert window // r == blk
        for s in range(r):
            for j in range(tt // (blk * r)):
                q2 = rows_of(cur_refs[bi], 0, r, s, j * blk, blk)
                k2 = keys(bi, 1, r, s, j)
                v2 = keys(bi, 2, r, s, j)
                o, lse = _attn_unit(q2, k2, v2, bias_ref[(2 if r == 1 else 0) + (1 if j == 0 else 0)], lo)
                if r == 1:
                    for c in range(4):
                        rows = pl.ds(j * blk + c, sub, stride=4)
                        ob_ref[bi, rows, :] = o[c * sub:(c + 1) * sub]
                        lb_ref[bi, rows, :] = lse[c * sub:(c + 1) * sub]
                elif r == 16:
                    rows = pl.ds((s % 4) * (tt // 4) + s // 4, blk, stride=4)
                    tb_ref[0, rows, :] = o
                    tb_ref[1, rows, :] = lse
                else:
                    rows = pl.ds(s + r * blk * j, blk, stride=r)
                    ob_ref[bi, rows, :] = o
                    lb_ref[bi, rows, :] = lse
        if r == 16:
            for s4 in range(4):
                src = slice(s4 * (tt // 4), (s4 + 1) * (tt // 4))
                ob_ref[bi, pl.ds(s4, tt // 4, stride=4), :] = tb_ref[0, src, :]
                lb_ref[bi, pl.ds(s4, tt // 4, stride=4), :] = tb_ref[1, src, :]

    rc = 256
    li = lax.broadcasted_iota(jnp.int32, (LANES, LANES), 0) // ATTN_HEAD_DIM
    lj = lax.broadcasted_iota(jnp.int32, (LANES, LANES), 1) // ATTN_HEAD_DIM
    head_mean = jnp.where(li == lj, 1.0 / ATTN_HEAD_DIM, 0.0).astype(BF16)
    gain = g_ref[...]

    for ci in range(tt // rc):
        rows = slice(ci * rc, (ci + 1) * rc)
        l0, l1, l2 = lb_ref[0, rows, :], lb_ref[1, rows, :], lb_ref[2, rows, :]
        mx = jnp.maximum(jnp.maximum(l0, l1), l2)
        e0, e1, e2 = jnp.exp2(l0 - mx), jnp.exp2(l1 - mx), jnp.exp2(l2 - mx)
        num = e0 * ob_ref[0, rows, :] + e1 * ob_ref[1, rows, :] + e2 * ob_ref[2, rows, :]
        den = e0 + e1 + e2
        ms = jnp.dot((num * num).astype(BF16), head_mean, preferred_element_type=F32)
        o_ref[rows, :] = (num * lax.rsqrt(ms + EPS * (den * den)) * gain).astype(BF16)


def _dilated_attention(layouts, gain):
    assert [r for _, r in DILATED_PATTERNS] == [1, 4, 16]
    bsz, r4, t4, w3 = layouts[0].shape
    t, w = r4 * t4, w3 // 3
    tt, blk = ATTN_TT, ATTN_BLOCK
    ngrp = w // LANES
    gw = 3 * LANES

    def cur_prev(r):
        rows = tt // r
        cur = pl.BlockSpec((None, r, rows, gw), lambda b, hp, i: (b, 0, i, hp))
        prev = pl.BlockSpec((None, r, blk, gw),
                            lambda b, hp, i: (b, 0, jnp.maximum(i * (rows // blk) - 1, 0), hp))
        return [cur, prev]

    return pl.pallas_call(
        _attn_kernel,
        out_shape=jax.ShapeDtypeStruct((bsz, t, w), BF16),
        grid=(bsz, ngrp, t // tt),
        in_specs=[sp for _, r in DILATED_PATTERNS[1:] for sp in cur_prev(r)]
        + [pl.BlockSpec((1, LANES), lambda b, hp, i: (0, hp))],
        out_specs=pl.BlockSpec((None, tt, LANES), lambda b, hp, i: (b, i, hp)),
        scratch_shapes=[
            pltpu.VMEM((4, blk, 2 * blk), F32),
            pltpu.VMEM((3, tt, LANES), F32), pltpu.VMEM((3, tt, LANES), F32), pltpu.VMEM((2, tt, LANES), F32),
        ],
        compiler_params=pltpu.CompilerParams(
            dimension_semantics=("arbitrary", "arbitrary", "arbitrary"), vmem_limit_bytes=VMEM_LIMIT),
        name="dilated_attention",
    )(*[a for a in layouts for _ in range(2)], gain)


def kernel(x, c, w_ada, b_ada, norm1_g, ffn1_w_in, ffn1_w_out, norm_mix_g, w_in_mix, ret_gn_g, attn_norm_g,
           w_out_mix, norm2_g, ffn2_w_in, ffn2_w_out, norm_f_g):
    depth = w_ada.shape[0]
    assert depth >= 1, "the final RMSNorm is fused into the last layer's second FFN"
    bsz, t, d = x.shape
    gf = norm_f_g.reshape(1, d)
    for l in range(depth):
        last = l == depth - 1
        mod = _adaln_mod(c, w_ada[l], b_ada[l]).reshape(bsz, N_MOD, d)
        x, (w_in_mix_b, w_out_mix_b, ffn2_w_in_b, ffn2_w_out_b) = _ffn(
            x, mod, norm1_g[l].reshape(1, d), ffn1_w_in[l].astype(BF16), ffn1_w_out[l].astype(BF16), gf,
            sub=0, final_norm=False, cast=(w_in_mix[l], w_out_mix[l], ffn2_w_in[l], ffn2_w_out[l]))
        ret_qkvg, *att_layouts = _mixer_in_proj(x, mod, norm_mix_g[l].reshape(1, d), w_in_mix_b)
        ret = _retention(ret_qkvg, ret_gn_g[l].reshape(1, RET_WIDTH))
        att = _dilated_attention(att_layouts, attn_norm_g[l].reshape(1, ATTN_WIDTH))
        x = _ffn(x, mod, norm2_g[l].reshape(1, d), ffn2_w_in_b, ffn2_w_out_b, gf,
                 sub=2, final_norm=last, mixer=(ret, att, w_out_mix_b))
    return x
```
